```python
import math
import jax
import jax.numpy as jnp
from jax import lax
import numpy as np

D_MODEL = 2048
BATCH = 2
SEQ = 4096
DEPTH = 4

GRID_W = 64
CTX_LEN = 256
N_MOD = 9
FFN_DIM = 5632
SSD_HEADS = 32
SSD_HEAD_DIM = 64
SSD_WIDTH = SSD_HEADS * SSD_HEAD_DIM
SSD_GROUPS = 8
SSD_STATE = 128
SSD_CONV = 3
SSD_CHUNK = 128
SSD_XBC = SSD_WIDTH + 2 * SSD_GROUPS * SSD_STATE
CM_WIDTH = 2048
CM_CONV = 31
NA_HEADS = 16
NA_HEAD_DIM = 128
NA_WIDTH = NA_HEADS * NA_HEAD_DIM
NA_ROWS = 8
NA_COLS = 16
SC_WIDTH = 2048
SC_CONV = 3
EVEN_IN = SSD_WIDTH + SSD_XBC + 2 * SSD_HEADS + 2 * CM_WIDTH
EVEN_OUT = SSD_WIDTH + CM_WIDTH
ODD_IN = 3 * NA_WIDTH + 3 * SC_WIDTH
ODD_OUT = NA_WIDTH + SC_WIDTH
EPS = 1e-6

kernel_name = 'hybrid_ssd_conformer_natten_shortconv_dit'


def _offsets(sizes):
    out, acc = [], 0
    for s in sizes[:-1]:
        acc += s
        out.append(acc)
    return out


def _rms(x):
    x32 = x.astype(jnp.float32)
    return (x32 * lax.rsqrt(jnp.mean(x32 * x32, axis=-1, keepdims=True) + EPS)).astype(x.dtype)


def rmsnorm(x, g):
    return _rms(x) * g.astype(x.dtype)


def layernorm(x, g, b):
    x32 = x.astype(jnp.float32)
    mu = jnp.mean(x32, axis=-1, keepdims=True)
    xc = x32 - mu
    y = xc * lax.rsqrt(jnp.mean(xc * xc, axis=-1, keepdims=True) + EPS)
    return (y * g.astype(jnp.float32) + b.astype(jnp.float32)).astype(x.dtype)


def adaln(h, g, shift, scale):
    return rmsnorm(h, g) * (1 + scale) + shift


def dwconv(x, w, b=None):
    k, ch = w.shape
    y = lax.conv_general_dilated(x, w[:, None, :].astype(x.dtype), window_strides=(1,),
                                 padding=[((k - 1) // 2, k // 2)],
                                 dimension_numbers=('NWC', 'WIO', 'NWC'), feature_group_count=ch)
    return y if b is None else y + b.astype(y.dtype)


def swiglu(h, w_in, w_out):
    a, g = jnp.split(h @ w_in, 2, axis=-1)
    return (jax.nn.silu(a) * g) @ w_out


def ffn_half(h, m, g, w_in, w_out, j):
    y = swiglu(adaln(h, g, m[:, :, 3 * j], m[:, :, 3 * j + 1]), w_in, w_out)
    return h + 0.5 * m[:, :, 3 * j + 2] * y


def ssd_scan(xs, dt, a_neg, bm, cm, h0, with_y=True):
    bsz, seq, nh, hp = xs.shape
    ng, ns = bm.shape[2], bm.shape[3]
    nr = nh // ng
    nc = seq // SSD_CHUNK
    xs_c = xs.reshape(bsz, nc, SSD_CHUNK, ng, nr, hp)
    dt_c = dt.reshape(bsz, nc, SSD_CHUNK, ng, nr)
    b_c = bm.reshape(bsz, nc, SSD_CHUNK, ng, ns)
    c_c = cm.reshape(bsz, nc, SSD_CHUNK, ng, ns)
    a_cum = jnp.cumsum(dt_c * a_neg.reshape(ng, nr), axis=2)
    a_last = a_cum[:, :, -1]
    w_state = jnp.exp(a_last[:, :, None] - a_cum) * dt_c
    states = jnp.einsum('bcqgn,bcqgr,bcqgrp->bcgrpn', b_c, w_state, xs_c)

    def step(h, inp):
        s, dec = inp
        return h * dec[..., None, None] + s, h

    h_last, h_prev = lax.scan(step, h0, (jnp.moveaxis(states, 1, 0), jnp.moveaxis(jnp.exp(a_last), 1, 0)))
    if not with_y:
        return None, h_last
    h_prev = jnp.moveaxis(h_prev, 0, 1)
    causal = jnp.tril(jnp.ones((SSD_CHUNK, SSD_CHUNK), dtype=bool))[:, :, None, None]
    seg = a_cum[:, :, :, None] - a_cum[:, :, None]
    decay = jnp.exp(jnp.where(causal, seg, -jnp.inf))
    cb = jnp.einsum('bcqgn,bckgn->bcqkg', c_c, b_c).astype(jnp.float32)
    mix = cb[..., None] * decay * dt_c[:, :, None]
    y_diag = jnp.einsum('bcqkgr,bckgrp->bcqgrp', mix, xs_c)
    y_off = jnp.einsum('bcqgn,bcgrpn->bcqgrp', c_c, h_prev) * jnp.exp(a_cum)[..., None]
    y = (y_diag + y_off).reshape(bsz, seq, nh, hp).astype(xs.dtype)
    return y, h_last


def _even_prep(h, w_in, conv_w, conv_b, dt_bias):
    bsz, seq, _ = h.shape
    z, xbc, dtf, dtb, ga, gg = jnp.split(
        h @ w_in, _offsets([SSD_WIDTH, SSD_XBC, SSD_HEADS, SSD_HEADS, CM_WIDTH, CM_WIDTH]), axis=-1)
    xbc = jax.nn.silu(dwconv(xbc, conv_w, conv_b))
    xs, bm, cm = jnp.split(xbc, _offsets([SSD_WIDTH, SSD_GROUPS * SSD_STATE, SSD_GROUPS * SSD_STATE]), axis=-1)
    xs = xs.reshape(bsz, seq, SSD_HEADS, SSD_HEAD_DIM)
    bm = bm.reshape(bsz, seq, SSD_GROUPS, SSD_STATE)
    cm = cm.reshape(bsz, seq, SSD_GROUPS, SSD_STATE)
    dtf = jax.nn.softplus(dtf.astype(jnp.float32) + dt_bias[0].astype(jnp.float32))
    dtb = jax.nn.softplus(dtb.astype(jnp.float32) + dt_bias[1].astype(jnp.float32))
    u = ga * jax.nn.sigmoid(gg)
    return z, xs, bm, cm, dtf, dtb, u


def _even_out(z, y, xs, u, d_skip, norm_g, cm_w, cm_b, cm_g, cm_beta, w_out):
    bsz, seq = z.shape[:2]
    y = y + d_skip[:, None].astype(y.dtype) * xs
    y = y.reshape(bsz, seq, SSD_WIDTH) * jax.nn.silu(z)
    y = _rms(y.reshape(bsz, seq, SSD_GROUPS, SSD_WIDTH // SSD_GROUPS)).reshape(bsz, seq, SSD_WIDTH) * norm_g
    v = jax.nn.silu(layernorm(dwconv(u, cm_w, cm_b), cm_g, cm_beta))
    return jnp.concatenate([y, v.astype(y.dtype)], axis=-1) @ w_out


def mixer_even(hx, hc, w_in, w_out, conv_w, conv_b, dt_bias, a_log, d_skip, norm_g,
               cm_w, cm_b, cm_g, cm_beta, need_ctx):
    zx, xsx, bx, cmx, dfx, dbx, ux = _even_prep(hx, w_in, conv_w, conv_b, dt_bias)
    zc, xsc, bc, cmc, dfc, dbc, uc = _even_prep(hc, w_in, conv_w, conv_b, dt_bias)
    a_neg = -jnp.exp(a_log.astype(jnp.float32))
    h0 = jnp.zeros((hx.shape[0], SSD_GROUPS, SSD_HEADS // SSD_GROUPS, SSD_HEAD_DIM, SSD_STATE), jnp.float32)
    rev = lambda t: t[:, ::-1]
    yfc, hf = ssd_scan(xsc, dfc, a_neg[0], bc, cmc, h0, need_ctx)
    yfx, _ = ssd_scan(xsx, dfx, a_neg[0], bx, cmx, hf)
    ybc, hb = ssd_scan(rev(xsc), rev(dbc), a_neg[1], rev(bc), rev(cmc), h0, need_ctx)
    ybx, _ = ssd_scan(rev(xsx), rev(dbx), a_neg[1], rev(bx), rev(cmx), hb)
    out_x = _even_out(zx, yfx + rev(ybx), xsx, ux, d_skip, norm_g, cm_w, cm_b, cm_g, cm_beta, w_out)
    if not need_ctx:
        return out_x, None
    out_c = _even_out(zc, yfc + rev(ybc), xsc, uc, d_skip, norm_g, cm_w, cm_b, cm_g, cm_beta, w_out)
    return out_x, out_c


def neighbourhood_attention(q, k, v, k_ctx, v_ctx, rpb):
    bsz, seq, nh, hd = q.shape
    rows = seq // GRID_W
    kr = min(NA_ROWS, rows)
    nk = kr * GRID_W
    col = jnp.arange(GRID_W)
    c0 = jnp.clip(col - NA_COLS // 2, 0, GRID_W - NA_COLS)
    col_ok = (col[None, :] >= c0[:, None]) & (col[None, :] < c0[:, None] + NA_COLS)
    mask = jnp.tile(col_ok, (1, kr))
    col_idx = jnp.clip(col[None, :] - col[:, None] + NA_COLS - 1, 0, 2 * NA_COLS - 2)
    col_bias = rpb[:, :, col_idx].astype(jnp.float32)
    q_rows = jnp.moveaxis(q.reshape(bsz, rows, GRID_W, nh, hd), 1, 0)

    def one_row(args):
        r, q_r = args
        r0 = jnp.clip(r - kr // 2, 0, rows - kr)
        k_w = lax.dynamic_slice_in_dim(k, r0 * GRID_W, nk, axis=1)
        v_w = lax.dynamic_slice_in_dim(v, r0 * GRID_W, nk, axis=1)
        bias = col_bias[:, r0 + jnp.arange(kr) - r + NA_ROWS - 1]
        bias = jnp.transpose(bias, (0, 2, 1, 3)).reshape(nh, GRID_W, nk)
        s_lat = jnp.einsum('bqhd,bkhd->bhqk', q_r, k_w).astype(jnp.float32) + bias
        s_lat = jnp.where(mask, s_lat, -jnp.inf)
        s_ctx = jnp.einsum('bqhd,bkhd->bhqk', q_r, k_ctx).astype(jnp.float32)
        p = jax.nn.softmax(jnp.concatenate([s_lat, s_ctx], axis=-1), axis=-1).astype(v.dtype)
        return (jnp.einsum('bhqk,bkhd->bqhd', p[..., :nk], v_w)
                + jnp.einsum('bhqk,bkhd->bqhd', p[..., nk:], v_ctx))

    o = lax.map(one_row, (jnp.arange(rows), q_rows))
    return jnp.moveaxis(o, 0, 1).reshape(bsz, seq, nh, hd)


def context_attention(q, k, v):
    s = jnp.einsum('bqhd,bkhd->bhqk', q, k).astype(jnp.float32)
    p = jax.nn.softmax(s, axis=-1).astype(v.dtype)
    return jnp.einsum('bhqk,bkhd->bqhd', p, v)


def _odd_prep(h, w_in, q_g, k_g):
    bsz, seq, _ = h.shape
    q, k, v, gb, gc, hs = jnp.split(h @ w_in, _offsets([NA_WIDTH] * 3 + [SC_WIDTH] * 3), axis=-1)
    heads = (bsz, seq, NA_HEADS, NA_HEAD_DIM)
    q = rmsnorm(q.reshape(heads), q_g) * (NA_HEAD_DIM ** -0.5)
    k = rmsnorm(k.reshape(heads), k_g)
    return q, k, v.reshape(heads), gb, gc, hs


def mixer_odd(hx, hc, w_in, w_out, q_g, k_g, rpb, sc_w, need_ctx):
    qx, kx, vx, bx, gx, sx = _odd_prep(hx, w_in, q_g, k_g)
    qc, kc, vc, bc, gc, sc = _odd_prep(hc, w_in, q_g, k_g)
    bsz, seq = hx.shape[:2]
    ox = neighbourhood_attention(qx, kx, vx, kc, vc, rpb).reshape(bsz, seq, NA_WIDTH)
    yx = bx * dwconv(gx * sx, sc_w)
    out_x = jnp.concatenate([ox, yx], axis=-1) @ w_out
    if not need_ctx:
        return out_x, None
    oc = context_attention(qc, kc, vc).reshape(bsz, hc.shape[1], NA_WIDTH)
    yc = bc * dwconv(gc * sc, sc_w)
    out_c = jnp.concatenate([oc, yc], axis=-1) @ w_out
    return out_x, out_c


def setup_inputs(seed: int = 0) -> dict:
    key = jax.random.key(seed)
    ks = iter(jax.random.split(key, 40))
    nrm = lambda shape, s: jax.random.normal(next(ks), shape, jnp.float32) * s
    n_even = (DEPTH + 1) // 2
    n_odd = DEPTH // 2
    dt0 = jnp.exp(jax.random.uniform(next(ks), (n_even, 2, SSD_HEADS), jnp.float32,
                                     minval=math.log(1e-3), maxval=math.log(1e-1)))
    dt_bias = dt0 + jnp.log(-jnp.expm1(-dt0))
    a_log = jnp.log(jax.random.uniform(next(ks), (n_even, 2, SSD_HEADS), jnp.float32, minval=1.0, maxval=16.0))
    return {
        'x': nrm((BATCH, SEQ, D_MODEL), 1.0),
        'c': nrm((BATCH, D_MODEL), 1.0),
        'ctx': nrm((BATCH, CTX_LEN, D_MODEL), 1.0),
        'c_ctx': nrm((D_MODEL,), 1.0),
        'w_mod': nrm((DEPTH, D_MODEL, N_MOD * D_MODEL), 0.5 * D_MODEL ** -0.5),
        'b_mod': nrm((DEPTH, N_MOD * D_MODEL), 0.02),
        'norm_g': 1.0 + nrm((DEPTH, 3, D_MODEL), 0.02),
        'ffn_w_in': nrm((DEPTH, 2, D_MODEL, 2 * FFN_DIM), D_MODEL ** -0.5),
        'ffn_w_out': nrm((DEPTH, 2, FFN_DIM, D_MODEL), FFN_DIM ** -0.5),
        'ev_w_in': nrm((n_even, D_MODEL, EVEN_IN), D_MODEL ** -0.5),
        'ev_w_out': nrm((n_even, EVEN_OUT, D_MODEL), EVEN_OUT ** -0.5),
        'ssd_conv_w': nrm((n_even, SSD_CONV, SSD_XBC), SSD_CONV ** -0.5),
        'ssd_conv_b': nrm((n_even, SSD_XBC), 0.02),
        'ssd_dt_bias': dt_bias,
        'ssd_a_log': a_log,
        'ssd_d': 1.0 + nrm((n_even, SSD_HEADS), 0.1),
        'ssd_norm_g': 1.0 + nrm((n_even, SSD_WIDTH), 0.02),
        'cm_conv_w': nrm((n_even, CM_CONV, CM_WIDTH), CM_CONV ** -0.5),
        'cm_conv_b': nrm((n_even, CM_WIDTH), 0.02),
        'cm_ln_g': 1.0 + nrm((n_even, CM_WIDTH), 0.02),
        'cm_ln_b': nrm((n_even, CM_WIDTH), 0.02),
        'od_w_in': nrm((n_odd, D_MODEL, ODD_IN), D_MODEL ** -0.5),
        'od_w_out': nrm((n_odd, ODD_OUT, D_MODEL), ODD_OUT ** -0.5),
        'na_q_g': 1.0 + nrm((n_odd, NA_HEAD_DIM), 0.02),
        'na_k_g': 1.0 + nrm((n_odd, NA_HEAD_DIM), 0.02),
        'na_rpb': nrm((n_odd, NA_HEADS, 2 * NA_ROWS - 1, 2 * NA_COLS - 1), 0.05),
        'sc_conv_w': nrm((n_odd, SC_CONV, SC_WIDTH), SC_CONV ** -0.5),
    }


def reference(x, c, ctx, c_ctx, w_mod, b_mod, norm_g, ffn_w_in, ffn_w_out, ev_w_in, ev_w_out,
              ssd_conv_w, ssd_conv_b, ssd_dt_bias, ssd_a_log, ssd_d, ssd_norm_g,
              cm_conv_w, cm_conv_b, cm_ln_g, cm_ln_b, od_w_in, od_w_out, na_q_g, na_k_g, na_rpb, sc_conv_w):
    bsz = x.shape[0]
    cx = ctx
    for i in range(DEPTH):
        last = i == DEPTH - 1
        mx = (jax.nn.silu(c) @ w_mod[i] + b_mod[i]).reshape(bsz, 1, N_MOD, D_MODEL)
        mc = (jax.nn.silu(c_ctx) @ w_mod[i] + b_mod[i]).reshape(1, 1, N_MOD, D_MODEL)
        x = ffn_half(x, mx, norm_g[i, 0], ffn_w_in[i, 0], ffn_w_out[i, 0], 0)
        cx = ffn_half(cx, mc, norm_g[i, 0], ffn_w_in[i, 0], ffn_w_out[i, 0], 0)
        hx = adaln(x, norm_g[i, 1], mx[:, :, 3], mx[:, :, 4])
        hc = adaln(cx, norm_g[i, 1], mc[:, :, 3], mc[:, :, 4])
        j = i // 2
        if i % 2 == 0:
            ox, oc = mixer_even(hx, hc, ev_w_in[j], ev_w_out[j], ssd_conv_w[j], ssd_conv_b[j], ssd_dt_bias[j],
                                ssd_a_log[j], ssd_d[j], ssd_norm_g[j], cm_conv_w[j], cm_conv_b[j],
                                cm_ln_g[j], cm_ln_b[j], not last)
        else:
            ox, oc = mixer_odd(hx, hc, od_w_in[j], od_w_out[j], na_q_g[j], na_k_g[j], na_rpb[j],
                               sc_conv_w[j], not last)
        x = x + mx[:, :, 5] * ox
        x = ffn_half(x, mx, norm_g[i, 2], ffn_w_in[i, 1], ffn_w_out[i, 1], 2)
        if not last:
            cx = cx + mc[:, :, 5] * oc
            cx = ffn_half(cx, mc, norm_g[i, 2], ffn_w_in[i, 1], ffn_w_out[i, 1], 2)
    return x
```

```python
import functools

import numpy as np
import jax
import jax.numpy as jnp
from jax import lax
from jax.experimental import pallas as pl
from jax.experimental.pallas import tpu as pltpu

F32 = jnp.float32
BF16 = jnp.bfloat16

D_MODEL = 2048
DEPTH = 4
GRID_W = 64
N_MOD = 9
FFN_DIM = 5632
SSD_HEADS = 32
SSD_HEAD_DIM = 64
SSD_WIDTH = SSD_HEADS * SSD_HEAD_DIM
SSD_GROUPS = 8
SSD_STATE = 128
SSD_CHUNK = 128
SSD_GN = SSD_GROUPS * SSD_STATE
SSD_XBC = SSD_WIDTH + 2 * SSD_GN
SSD_HPG = SSD_HEADS // SSD_GROUPS
SSD_GW = SSD_HPG * SSD_HEAD_DIM
CM_WIDTH = 2048
CM_CONV = 31
NA_HEADS = 16
NA_HEAD_DIM = 128
NA_WIDTH = NA_HEADS * NA_HEAD_DIM
NA_ROWS = 8
NA_COLS = 16
SC_WIDTH = 2048
EPS = 1e-6

NA_QROWS = 8
NA_KROWS = 16
NEG_BIG = -1e30

VMEM_LIMIT = 56 * 1024 * 1024


def _cparams(sem):
    return pltpu.CompilerParams(dimension_semantics=sem, vmem_limit_bytes=VMEM_LIMIT)


def _silu(v):
    return v * jax.nn.sigmoid(v)


def _mod_kernel(c_ref, w_ref, b_ref, o_ref):
    s = _silu(c_ref[...]).astype(BF16)
    o_ref[0] = jnp.dot(s, w_ref[0].astype(BF16), preferred_element_type=F32) + b_ref[0]


def modulation(cvec, w_mod, b_mod, tn=1024):
    depth, d, n = w_mod.shape
    rows = cvec.shape[0]
    return pl.pallas_call(
        _mod_kernel,
        grid=(depth, n // tn),
        in_specs=[pl.BlockSpec((rows, d), lambda l, j: (0, 0)),
                  pl.BlockSpec((1, d, tn), lambda l, j: (l, 0, j)),
                  pl.BlockSpec((1, 1, tn), lambda l, j: (l, 0, j))],
        out_specs=pl.BlockSpec((1, rows, tn), lambda l, j: (l, 0, j)),
        out_shape=jax.ShapeDtypeStruct((depth, rows, n), F32),
        compiler_params=_cparams(("parallel", "parallel")),
        name="modulation",
    )(cvec, w_mod, b_mod.reshape(depth, 1, n))


NORM_ROWS = 16


def _nm_kernel(n_w, n_extra, epilogue, tm, x_ref, sh_ref, sc_ref, g_ref, *rest):
    w_refs = rest[:n_w]
    extra = rest[n_w:n_w + n_extra]
    o_ref = rest[n_w + n_extra]
    xn_ref = rest[n_w + n_extra + 1]

    @pl.when(pl.program_id(1) == 0)
    def _():
        def body(r, carry):
            rows = pl.ds(pl.multiple_of(r * NORM_ROWS, NORM_ROWS), NORM_ROWS)
            xv = x_ref[rows, :]
            ms = jnp.mean(xv * xv, axis=-1, keepdims=True)
            y = xv * lax.rsqrt(ms + EPS) * g_ref[...]
            y = y * (1.0 + sc_ref[0]) + sh_ref[0]
            xn_ref[rows, :] = y.astype(BF16)
            return carry
        lax.fori_loop(0, tm // NORM_ROWS, body, 0)

    xn = xn_ref[...]
    accs = [jnp.dot(xn, w[...], preferred_element_type=F32) for w in w_refs]
    epilogue(accs, extra, o_ref)


def norm_matmul(x, shift, scale, g, ws, col_offs, n_out, epilogue, out_dtype, *, rows_per_mod, tm, tn,
                extras=(), name="norm_matmul"):
    t, d = x.shape
    assert t % tm == 0 and n_out % tn == 0 and rows_per_mod % tm == 0
    for off in col_offs:
        assert off % tn == 0
    mod_spec = pl.BlockSpec((1, 1, d), lambda i, j: ((i * tm) // rows_per_mod, 0, 0))
    in_specs = [pl.BlockSpec((tm, d), lambda i, j: (i, 0)), mod_spec, mod_spec,
                pl.BlockSpec((1, d), lambda i, j: (0, 0))]
    for off in col_offs:
        in_specs.append(pl.BlockSpec((d, tn), functools.partial(lambda i, j, o: (0, j + o), o=off // tn)))
    for e in extras:
        if e.shape[1] == n_out:
            in_specs.append(pl.BlockSpec((1, tn), lambda i, j: (0, j)))
        else:
            in_specs.append(pl.BlockSpec(e.shape, lambda i, j: (0, 0)))
    kern = functools.partial(_nm_kernel, len(ws), len(extras), epilogue, tm)
    return pl.pallas_call(
        kern,
        grid=(t // tm, n_out // tn),
        in_specs=in_specs,
        out_specs=pl.BlockSpec((tm, tn), lambda i, j: (i, j)),
        out_shape=jax.ShapeDtypeStruct((t, n_out), out_dtype),
        scratch_shapes=[pltpu.VMEM((tm, d), BF16)],
        compiler_params=_cparams(("parallel", "arbitrary")),
        name=name,
    )(x, shift, scale, g, *ws, *extras)


def _ep_swiglu(accs, extra, o_ref):
    a, g = accs
    o_ref[...] = (_silu(a) * g).astype(o_ref.dtype)


def _ep_glu_sigmoid(accs, extra, o_ref):
    a, g = accs
    o_ref[...] = (a * jax.nn.sigmoid(g)).astype(o_ref.dtype)


def _ep_mul(accs, extra, o_ref):
    a, g = accs
    o_ref[...] = (a * g).astype(o_ref.dtype)


def _ep_silu(accs, extra, o_ref):
    o_ref[...] = _silu(accs[0]).astype(o_ref.dtype)


def _ep_id(accs, extra, o_ref):
    o_ref[...] = accs[0].astype(o_ref.dtype)


def _ep_softplus_bias(accs, extra, o_ref):
    v = accs[0] + extra[0][...]
    o_ref[...] = jnp.maximum(v, 0.0) + jnp.log1p(jnp.exp(-jnp.abs(v)))


def _ep_head_rms(post_scale, accs, extra, o_ref):
    a = accs[0]
    hg = extra[0][...]
    for h in range(a.shape[1] // NA_HEAD_DIM):
        sl = slice(h * NA_HEAD_DIM, (h + 1) * NA_HEAD_DIM)
        ah = a[:, sl]
        ms = jnp.mean(ah * ah, axis=-1, keepdims=True)
        y = ah * lax.rsqrt(ms + EPS) * hg
        if post_scale is not None:
            y = y * post_scale
        o_ref[:, sl] = y.astype(o_ref.dtype)


def _mr_kernel(n_p, coef, *refs):
    lhs = refs[:n_p]
    ws = refs[n_p:2 * n_p]
    x_ref, gate_ref, o_ref = refs[2 * n_p:]
    acc = jnp.dot(lhs[0][...], ws[0][...], preferred_element_type=F32)
    for p in range(1, n_p):
        acc = acc + jnp.dot(lhs[p][...], ws[p][...], preferred_element_type=F32)
    gate = gate_ref[0]
    if coef != 1.0:
        gate = coef * gate
    o_ref[...] = x_ref[...] + gate * acc


def matmul_residual(lhs_list, w_list, lhs_offs, x, gate, coef, *, rows_per_mod, tm, tn, name="matmul_residual"):
    t, d = x.shape
    assert t % tm == 0 and d % tn == 0 and rows_per_mod % tm == 0
    in_specs = []
    for l, w, off in zip(lhs_list, w_list, lhs_offs):
        kp = w.shape[0]
        assert off % kp == 0
        in_specs.append(pl.BlockSpec((tm, kp), functools.partial(lambda i, j, o: (i, o), o=off // kp)))
    for w in w_list:
        in_specs.append(pl.BlockSpec((w.shape[0], tn), lambda i, j: (0, j)))
    in_specs.append(pl.BlockSpec((tm, tn), lambda i, j: (i, j)))
    in_specs.append(pl.BlockSpec((1, 1, tn), lambda i, j: ((i * tm) // rows_per_mod, 0, j)))
    return pl.pallas_call(
        functools.partial(_mr_kernel, len(lhs_list), coef),
        grid=(t // tm, d // tn),
        in_specs=in_specs,
        out_specs=pl.BlockSpec((tm, tn), lambda i, j: (i, j)),
        out_shape=jax.ShapeDtypeStruct((t, d), F32),
        compiler_params=_cparams(("parallel", "parallel")),
        name=name,
    )(*lhs_list, *w_list, x, gate)


def _shift_rows(v, s):
    n = v.shape[0]
    r = pltpu.roll(v, s % n, axis=0)
    row = lax.broadcasted_iota(jnp.int32, v.shape, 0)
    if s > 0:
        return jnp.where(row < s, 0.0, r)
    return jnp.where(row >= n + s, 0.0, r)


def _conv3(v, w_ref):
    return (w_ref[0:1, :] * _shift_rows(v, 1) + w_ref[1:2, :] * v) + w_ref[2:3, :] * _shift_rows(v, -1)


def _conv3_silu_kernel(n_c, xc_ref, xx_ref, w_ref, b_ref, o_ref):
    for src, lo, ln in ((xc_ref, 0, n_c), (xx_ref, n_c, xx_ref.shape[1])):
        y = _conv3(src[0], w_ref) + b_ref[...]
        o_ref[0, lo:lo + ln, :] = _silu(y)


def conv3_silu_joint(xc, xx, w, b, ct=256):
    bsz, lc, ch = xc.shape
    lx = xx.shape[1]
    return pl.pallas_call(
        functools.partial(_conv3_silu_kernel, lc),
        grid=(bsz, ch // ct),
        in_specs=[pl.BlockSpec((1, lc, ct), lambda b_, j: (b_, 0, j)),
                  pl.BlockSpec((1, lx, ct), lambda b_, j: (b_, 0, j)),
                  pl.BlockSpec((3, ct), lambda b_, j: (0, j)),
                  pl.BlockSpec((1, ct), lambda b_, j: (0, j))],
        out_specs=pl.BlockSpec((1, lc + lx, ct), lambda b_, j: (b_, 0, j)),
        out_shape=jax.ShapeDtypeStruct((bsz, lc + lx, ch), F32),
        compiler_params=_cparams(("parallel", "parallel")),
        name="ssd_conv3_silu",
    )(xc, xx, w, b)


def _gated_conv3_kernel(p_ref, gate_ref, w_ref, o_ref):
    o_ref[0] = (gate_ref[0] * _conv3(p_ref[0], w_ref)).astype(o_ref.dtype)


def gated_conv3(prod, gate, w, ct=256):
    bsz, ln, ch = prod.shape
    spec = pl.BlockSpec((1, ln, ct), lambda b_, j: (b_, 0, j))
    return pl.pallas_call(
        _gated_conv3_kernel,
        grid=(bsz, ch // ct),
        in_specs=[spec, spec, pl.BlockSpec((3, ct), lambda b_, j: (0, j))],
        out_specs=spec,
        out_shape=jax.ShapeDtypeStruct((bsz, ln, ch), BF16),
        compiler_params=_cparams(("parallel", "parallel")),
        name="short_gated_conv",
    )(prod, gate, w)


CONV_HALO = 16
CONV_ROWS = 64
SUBLANES = 8


def _conv31_kernel(n_c, uc_ref, ux_ref, w_ref, b_ref, o_ref, sh_ref):
    ct = o_ref.shape[2]

    def run(src_ref, lo, ln):
        lp = ln + 2 * CONV_HALO
        zeros = jnp.zeros((CONV_HALO, ct), F32)
        sh_ref[0, 0:CONV_HALO, :] = zeros
        sh_ref[0, CONV_HALO:CONV_HALO + ln, :] = src_ref[0]
        sh_ref[0, CONV_HALO + ln:lp, :] = zeros
        p = sh_ref[0, 0:lp, :]
        for s in range(1, SUBLANES):
            sh_ref[s, 0:lp, :] = pltpu.roll(p, lp - s, axis=0)

        def body(ci, carry):
            r0 = pl.multiple_of(ci * CONV_ROWS, CONV_ROWS)
            acc = jnp.zeros((CONV_ROWS, ct), F32) + b_ref[...]
            for k in range(CM_CONV):
                a, s = divmod(k + CONV_HALO - (CM_CONV - 1) // 2, SUBLANES)
                start = pl.multiple_of(r0 + SUBLANES * a, SUBLANES)
                acc = acc + sh_ref[s, pl.ds(start, CONV_ROWS), :] * w_ref[k:k + 1, :]
            o_ref[0, pl.ds(pl.multiple_of(lo + r0, CONV_ROWS), CONV_ROWS), :] = acc
            return carry
        lax.fori_loop(0, ln // CONV_ROWS, body, 0)

    run(uc_ref, 0, n_c)
    run(ux_ref, n_c, ux_ref.shape[1])


def conv31_joint(uc, ux, w, b, ct=128):
    bsz, lc, ch = uc.shape
    lx = ux.shape[1]
    return pl.pallas_call(
        functools.partial(_conv31_kernel, lc),
        grid=(bsz, ch // ct),
        in_specs=[pl.BlockSpec((1, lc, ct), lambda b_, j: (b_, 0, j)),
                  pl.BlockSpec((1, lx, ct), lambda b_, j: (b_, 0, j)),
                  pl.BlockSpec((CM_CONV, ct), lambda b_, j: (0, j)),
                  pl.BlockSpec((1, ct), lambda b_, j: (0, j))],
        out_specs=pl.BlockSpec((1, lc + lx, ct), lambda b_, j: (b_, 0, j)),
        out_shape=jax.ShapeDtypeStruct((bsz, lc + lx, ch), F32),
        scratch_shapes=[pltpu.VMEM((SUBLANES, lx + 2 * CONV_HALO, ct), F32)],
        compiler_params=_cparams(("parallel", "parallel")),
        name="conformer_conv31",
    )(uc, ux, w, b)


def _ssd_kernel(xs_ref, b_ref, c_ref, dt_ref, dtt_ref, ar_ref, ac_ref, y_ref, state_ref):
    q = SSD_CHUNK
    sgn = 1 - 2 * pl.program_id(0)

    @pl.when(pl.program_id(2) == 0)
    def _():
        state_ref[...] = jnp.zeros(state_ref.shape, F32)

    dtc = dt_ref[0, 0]
    dtr = dtt_ref[0, 0]
    da_c = dtc * ar_ref[0]
    da_r = dtr * ac_ref[0]
    ri = lax.broadcasted_iota(jnp.int32, (q, q), 0)
    ci = lax.broadcasted_iota(jnp.int32, (q, q), 1)
    keep = (ci - ri) * sgn <= 0
    tri_c = jnp.where(keep, 1.0, 0.0)
    tri_r = jnp.where((ri - ci) * sgn <= 0, 1.0, 0.0)
    acum_c = jnp.dot(tri_c, da_c, preferred_element_type=F32, precision=lax.Precision.HIGHEST)
    acum_r = jnp.dot(da_r, tri_r, preferred_element_type=F32, precision=lax.Precision.HIGHEST)
    alast_c = jnp.sum(da_c, axis=0, keepdims=True)
    lane_head = lax.broadcasted_iota(jnp.int32, (q, SSD_GW), 1) // SSD_HEAD_DIM

    def per_head(cols):
        out = cols[-1]
        for r in reversed(range(SSD_HPG - 1)):
            out = jnp.where(lane_head == r, cols[r], out)
        return out

    for g in range(SSD_GROUPS):
        bg = b_ref[0, :, g * SSD_STATE:(g + 1) * SSD_STATE].astype(BF16)
        cg = c_ref[0, :, g * SSD_STATE:(g + 1) * SSD_STATE].astype(BF16)
        cb = lax.dot_general(cg, bg, (((1,), (1,)), ((), ())), preferred_element_type=F32)
        h_prev = state_ref[g]
        y_off = jnp.dot(cg, h_prev.astype(BF16), preferred_element_type=F32)
        gcol = slice(g * SSD_GW, (g + 1) * SSD_GW)
        xs_g = xs_ref[0, :, gcol]
        xs_b = xs_g.astype(BF16)
        y_d, e_ac, w_st, dec = [], [], [], []
        for r in range(SSD_HPG):
            h = g * SSD_HPG + r
            ac = acum_c[:, h:h + 1]
            ar = acum_r[h:h + 1, :]
            decay = jnp.exp(jnp.where(keep, ac - ar, NEG_BIG))
            mix = (cb * decay * dtr[h:h + 1, :]).astype(BF16)
            y_d.append(jnp.dot(mix, xs_b, preferred_element_type=F32))
            al = alast_c[:, h:h + 1]
            e_ac.append(jnp.exp(ac))
            w_st.append(jnp.exp(al - ac) * dtc[:, h:h + 1])
            dec.append(jnp.exp(al))
        y_ref[0, 0, :, gcol] = per_head(y_d) + y_off * per_head(e_ac)
        xw = (xs_g * per_head(w_st)).astype(BF16)
        s_new = lax.dot_general(bg, xw, (((0,), (0,)), ((), ())), preferred_element_type=F32)
        state_ref[g] = h_prev * per_head(dec) + s_new


def ssd_scan_joint(xbc, dt, dtt, a_row, a_col, n_ctx_chunks):
    bsz, ln, _ = xbc.shape
    nc = ln // SSD_CHUNK
    q = SSD_CHUNK

    def pos(d, c):
        back = jnp.where(c < n_ctx_chunks, n_ctx_chunks - 1 - c, nc + n_ctx_chunks - 1 - c)
        return jnp.where(d == 0, c, back)

    return pl.pallas_call(
        _ssd_kernel,
        grid=(2, bsz, nc),
        in_specs=[pl.BlockSpec((1, q, SSD_WIDTH), lambda d, b, c: (b, pos(d, c), 0)),
                  pl.BlockSpec((1, q, SSD_GN), lambda d, b, c: (b, pos(d, c), SSD_WIDTH // SSD_GN)),
                  pl.BlockSpec((1, q, SSD_GN), lambda d, b, c: (b, pos(d, c), SSD_WIDTH // SSD_GN + 1)),
                  pl.BlockSpec((1, 1, q, SSD_HEADS), lambda d, b, c: (d, b, pos(d, c), 0)),
                  pl.BlockSpec((1, 1, SSD_HEADS, q), lambda d, b, c: (d, b, 0, pos(d, c))),
                  pl.BlockSpec((1, 1, SSD_HEADS), lambda d, b, c: (d, 0, 0)),
                  pl.BlockSpec((1, SSD_HEADS, 1), lambda d, b, c: (d, 0, 0))],
        out_specs=pl.BlockSpec((1, 1, q, SSD_WIDTH), lambda d, b, c: (d, b, pos(d, c), 0)),
        out_shape=jax.ShapeDtypeStruct((2, bsz, ln, SSD_WIDTH), F32),
        scratch_shapes=[pltpu.VMEM((SSD_GROUPS, SSD_STATE, SSD_GW), F32)],
        compiler_params=_cparams(("parallel", "parallel", "arbitrary")),
        name="ssd_scan",
    )(xbc, xbc, xbc, dt, dtt, a_row, a_col)


EO_ROWS = 16


def _even_out_kernel(tq, yf_ref, yb_ref, xs_ref, zs_ref, cu_ref, dv_ref, ng_ref, lg_ref, lb_ref, o_ref):
    gw = SSD_WIDTH // SSD_GROUPS

    def body(r, carry):
        rows = pl.ds(pl.multiple_of(r * EO_ROWS, EO_ROWS), EO_ROWS)
        y = yf_ref[0, 0, rows, :] + yb_ref[0, 0, rows, :]
        y = y + dv_ref[...] * xs_ref[0, rows, :]
        y = y * zs_ref[rows, :]
        for g in range(SSD_GROUPS):
            sl = slice(g * gw, (g + 1) * gw)
            yg = y[:, sl]
            ms = jnp.mean(yg * yg, axis=-1, keepdims=True)
            o_ref[rows, sl] = (yg * lax.rsqrt(ms + EPS) * ng_ref[:, sl]).astype(o_ref.dtype)
        cv = cu_ref[0, rows, :]
        mu = jnp.mean(cv, axis=-1, keepdims=True)
        xc = cv - mu
        var = jnp.mean(xc * xc, axis=-1, keepdims=True)
        ln = xc * lax.rsqrt(var + EPS) * lg_ref[...] + lb_ref[...]
        o_ref[rows, SSD_WIDTH:SSD_WIDTH + CM_WIDTH] = _silu(ln).astype(o_ref.dtype)
        return carry
    lax.fori_loop(0, tq // EO_ROWS, body, 0)


def even_out(y, xbc, zs, cu, dvec, ng, lg, lb, *, blk0, nblk, tq):
    bsz = xbc.shape[0]
    vec = pl.BlockSpec((1, SSD_WIDTH), lambda b, i: (0, 0))
    return pl.pallas_call(
        functools.partial(_even_out_kernel, tq),
        grid=(bsz, nblk),
        in_specs=[pl.BlockSpec((1, 1, tq, SSD_WIDTH), lambda b, i: (0, b, blk0 + i, 0)),
                  pl.BlockSpec((1, 1, tq, SSD_WIDTH), lambda b, i: (1, b, blk0 + i, 0)),
                  pl.BlockSpec((1, tq, SSD_WIDTH), lambda b, i: (b, blk0 + i, 0)),
                  pl.BlockSpec((tq, SSD_WIDTH), lambda b, i: (b * nblk + i, 0)),
                  pl.BlockSpec((1, tq, CM_WIDTH), lambda b, i: (b, blk0 + i, 0)),
                  vec, vec, vec, vec],
        out_specs=pl.BlockSpec((tq, SSD_WIDTH + CM_WIDTH), lambda b, i: (b * nblk + i, 0)),
        out_shape=jax.ShapeDtypeStruct((bsz * nblk * tq, SSD_WIDTH + CM_WIDTH), BF16),
        compiler_params=_cparams(("parallel", "parallel")),
        name="even_mixer_tail",
    )(y, y, xbc, zs, cu, dvec, ng, lg, lb)


_NT = (((1,), (1,)), ((), ()))


def _na_kernel(rows_total, q_ref, k_ref, v_ref, kc_ref, vc_ref, bias_ref, o_ref):
    rb = pl.program_id(2)
    k0 = jnp.clip(rb * NA_QROWS - NA_ROWS // 2, 0, rows_total - NA_KROWS) * GRID_W
    k0 = pl.multiple_of(k0, GRID_W)
    nk = NA_KROWS * GRID_W
    qv = q_ref[...]
    s = lax.dot_general(qv, k_ref[pl.ds(k0, nk), :], _NT, preferred_element_type=F32) + bias_ref[0, 0]
    sc = lax.dot_general(qv, kc_ref[...], _NT, preferred_element_type=F32)
    m = jnp.maximum(jnp.max(s, axis=-1, keepdims=True), jnp.max(sc, axis=-1, keepdims=True))
    p = jnp.exp(s - m)
    pc = jnp.exp(sc - m)
    den = jnp.sum(p, axis=-1, keepdims=True) + jnp.sum(pc, axis=-1, keepdims=True)
    o = jnp.dot(p.astype(BF16), v_ref[pl.ds(k0, nk), :], preferred_element_type=F32)
    o = o + jnp.dot(pc.astype(BF16), vc_ref[...], preferred_element_type=F32)
    o_ref[...] = (o / den).astype(o_ref.dtype)


def neighbourhood_attention(q, k, v, kc, vc, bias, bsz, seq, n_ctx):
    rows = seq // GRID_W
    nrb = rows // NA_QROWS
    tq = NA_QROWS * GRID_W
    nk = NA_KROWS * GRID_W

    def btype(rb):
        return jnp.where(rb == 0, 0, jnp.where(rb == nrb - 1, 2, 1))

    return pl.pallas_call(
        functools.partial(_na_kernel, rows),
        grid=(NA_HEADS, bsz, nrb),
        in_specs=[pl.BlockSpec((tq, NA_HEAD_DIM), lambda h, b, r: (b * nrb + r, h)),
                  pl.BlockSpec((seq, NA_HEAD_DIM), lambda h, b, r: (b, h)),
                  pl.BlockSpec((seq, NA_HEAD_DIM), lambda h, b, r: (b, h)),
                  pl.BlockSpec((n_ctx, NA_HEAD_DIM), lambda h, b, r: (b, h)),
                  pl.BlockSpec((n_ctx, NA_HEAD_DIM), lambda h, b, r: (b, h)),
                  pl.BlockSpec((1, 1, tq, nk), lambda h, b, r: (btype(r), h, 0, 0))],
        out_specs=pl.BlockSpec((tq, NA_HEAD_DIM), lambda h, b, r: (b * nrb + r, h)),
        out_shape=jax.ShapeDtypeStruct((bsz * seq, NA_WIDTH), BF16),
        compiler_params=_cparams(("parallel", "parallel", "arbitrary")),
        name="neighbourhood_attention",
    )(q, k, v, kc, vc, bias)


def _ctx_attn_kernel(q_ref, k_ref, v_ref, o_ref):
    s = lax.dot_general(q_ref[...], k_ref[...], _NT, preferred_element_type=F32)
    m = jnp.max(s, axis=-1, keepdims=True)
    p = jnp.exp(s - m)
    den = jnp.sum(p, axis=-1, keepdims=True)
    o = jnp.dot(p.astype(BF16), v_ref[...], preferred_element_type=F32)
    o_ref[...] = (o / den).astype(o_ref.dtype)


def context_attention(q, k, v, bsz, n_ctx):
    spec = pl.BlockSpec((n_ctx, NA_HEAD_DIM), lambda b, h: (b, h))
    return pl.pallas_call(
        _ctx_attn_kernel,
        grid=(bsz, NA_HEADS),
        in_specs=[spec, spec, spec],
        out_specs=spec,
        out_shape=jax.ShapeDtypeStruct((bsz * n_ctx, NA_WIDTH), BF16),
        compiler_params=_cparams(("parallel", "parallel")),
        name="context_attention",
    )(q, k, v)


def _na_bias_tables(rpb, rows):
    nrb = rows // NA_QROWS
    col = np.arange(GRID_W)
    c0 = np.clip(col - NA_COLS // 2, 0, GRID_W - NA_COLS)
    col_ok = (col[None, :] >= c0[:, None]) & (col[None, :] < c0[:, None] + NA_COLS)
    col_idx = np.clip(col[None, :] - col[:, None] + NA_COLS - 1, 0, 2 * NA_COLS - 2)
    kr = min(NA_ROWS, rows)
    ridx = np.zeros((3, NA_QROWS, NA_KROWS), np.int32)
    rok = np.zeros((3, NA_QROWS, NA_KROWS), bool)
    for t, rb in enumerate((0, 1, nrb - 1)):
        k0 = int(np.clip(rb * NA_QROWS - NA_ROWS // 2, 0, rows - NA_KROWS))
        for rq in range(NA_QROWS):
            r = rb * NA_QROWS + rq
            r0 = int(np.clip(r - kr // 2, 0, rows - kr))
            for rk in range(NA_KROWS):
                ra = k0 + rk
                rok[t, rq, rk] = r0 <= ra < r0 + kr
                ridx[t, rq, rk] = int(np.clip(ra - r + NA_ROWS - 1, 0, 2 * NA_ROWS - 2))
    tz = rpb[:, :, col_idx].astype(F32)
    tz = jnp.where(col_ok[None, None], tz, NEG_BIG)
    tz = jnp.transpose(tz, (0, 2, 1, 3))
    big = tz[:, :, ridx, :]
    big = jnp.where(rok[None, None, :, :, :, None], big, NEG_BIG)
    big = jnp.transpose(big, (2, 0, 3, 1, 4, 5))
    return big.reshape(3, NA_HEADS, NA_QROWS * GRID_W, NA_KROWS * GRID_W)


class _Stream:
    def __init__(self, mod, rows_per_mod, tm):
        self.mod = mod
        self.rows_per_mod = rows_per_mod
        self.tm = tm

    def m(self, idx):
        return self.mod[:, idx][:, None, :]


def _ffn_half(h, st, g, w_in, w_out, j):
    hid = norm_matmul(h, st.m(3 * j), st.m(3 * j + 1), g, [w_in, w_in], [0, FFN_DIM], FFN_DIM, _ep_swiglu, BF16,
                      rows_per_mod=st.rows_per_mod, tm=st.tm, tn=512, name="ffn_in")
    return matmul_residual([hid], [w_out], [0], h, st.m(3 * j + 2), 0.5,
                           rows_per_mod=st.rows_per_mod, tm=st.tm, tn=512, name="ffn_out")


def _proj(h, st, g, ws, offs, n_out, ep, dtype, tn=512, extras=(), name="mixer_in"):
    return norm_matmul(h, st.m(3), st.m(4), g, ws, offs, n_out, ep, dtype,
                       rows_per_mod=st.rows_per_mod, tm=st.tm, tn=tn, extras=extras, name=name)


def _mixer_even(x, cx, sx, sc, g, p, bsz, seq, n_ctx):
    w = p["w_in"]
    outs = {}
    for key, h, st in (("x", x, sx), ("c", cx, sc)):
        zs = _proj(h, st, g, [w["z"]], [0], SSD_WIDTH, _ep_silu, F32, name="ssd_z")
        xbc = _proj(h, st, g, [w["xbc"]], [0], SSD_XBC, _ep_id, F32, name="ssd_xbc")
        dt = _proj(h, st, g, [w["dt"]], [0], 2 * SSD_HEADS, _ep_softplus_bias, F32, tn=2 * SSD_HEADS,
                   extras=(p["dt_bias"],), name="ssd_dt")
        u = _proj(h, st, g, [w["ga"], w["gg"]], [0, 0], CM_WIDTH, _ep_glu_sigmoid, F32, name="conformer_glu")
        outs[key] = (zs, xbc, dt, u)
    zs_x, xbc_x, dt_x, u_x = outs["x"]
    zs_c, xbc_c, dt_c, u_c = outs["c"]
    xbc = conv3_silu_joint(xbc_c.reshape(bsz, n_ctx, SSD_XBC), xbc_x.reshape(bsz, seq, SSD_XBC),
                           p["conv_w"], p["conv_b"])
    dt = jnp.concatenate([dt_c.reshape(bsz, n_ctx, 2, SSD_HEADS), dt_x.reshape(bsz, seq, 2, SSD_HEADS)], axis=1)
    dt = jnp.transpose(dt, (2, 0, 1, 3))
    dtt = jnp.transpose(dt, (0, 1, 3, 2))
    y = ssd_scan_joint(xbc, dt, dtt, p["a_row"], p["a_col"], n_ctx // SSD_CHUNK)
    cu = conv31_joint(u_c.reshape(bsz, n_ctx, CM_WIDTH), u_x.reshape(bsz, seq, CM_WIDTH), p["cm_w"], p["cm_b"])
    tq = n_ctx
    tail = functools.partial(even_out, y, xbc, dvec=p["dvec"], ng=p["norm_g"], lg=p["ln_g"], lb=p["ln_b"], tq=tq)
    lhs_x = tail(zs=zs_x, cu=cu, blk0=1, nblk=seq // tq)
    lhs_c = tail(zs=zs_c, cu=cu, blk0=0, nblk=1)
    return lhs_x, lhs_c


def _mixer_odd(x, cx, sx, sc, g, p, bsz, seq, n_ctx, need_ctx):
    w = p["w_in"]
    hg_q, hg_k = p["q_g"], p["k_g"]
    ep_q = functools.partial(_ep_head_rms, NA_HEAD_DIM ** -0.5)
    ep_k = functools.partial(_ep_head_rms, None)
    kc = _proj(cx, sc, g, [w["k"]], [0], NA_WIDTH, ep_k, BF16, extras=(hg_k,), name="na_k")
    vc = _proj(cx, sc, g, [w["v"]], [0], NA_WIDTH, _ep_id, BF16, name="na_v")
    q = _proj(x, sx, g, [w["q"]], [0], NA_WIDTH, ep_q, BF16, extras=(hg_q,), name="na_q")
    k = _proj(x, sx, g, [w["k"]], [0], NA_WIDTH, ep_k, BF16, extras=(hg_k,), name="na_k")
    v = _proj(x, sx, g, [w["v"]], [0], NA_WIDTH, _ep_id, BF16, name="na_v")
    gb = _proj(x, sx, g, [w["gb"]], [0], SC_WIDTH, _ep_id, F32, name="sc_gate")
    pr = _proj(x, sx, g, [w["gc"], w["hs"]], [0, 0], SC_WIDTH, _ep_mul, F32, name="sc_prod")
    ox = neighbourhood_attention(q, k, v, kc, vc, p["bias"], bsz, seq, n_ctx)
    yx = gated_conv3(pr.reshape(bsz, seq, SC_WIDTH), gb.reshape(bsz, seq, SC_WIDTH), p["sc_w"])
    yx = yx.reshape(bsz * seq, SC_WIDTH)
    if not need_ctx:
        return (ox, yx), None
    qc = _proj(cx, sc, g, [w["q"]], [0], NA_WIDTH, ep_q, BF16, extras=(hg_q,), name="na_q")
    gbc = _proj(cx, sc, g, [w["gb"]], [0], SC_WIDTH, _ep_id, F32, name="sc_gate")
    prc = _proj(cx, sc, g, [w["gc"], w["hs"]], [0, 0], SC_WIDTH, _ep_mul, F32, name="sc_prod")
    oc = context_attention(qc, kc, vc, bsz, n_ctx)
    yc = gated_conv3(prc.reshape(bsz, n_ctx, SC_WIDTH), gbc.reshape(bsz, n_ctx, SC_WIDTH), p["sc_w"])
    yc = yc.reshape(bsz * n_ctx, SC_WIDTH)
    return (ox, yx), (oc, yc)


def _split_cols(w, sizes):
    out, acc = [], 0
    for s in sizes:
        out.append(w[:, acc:acc + s].astype(BF16))
        acc += s
    return out


def kernel(x, c, ctx, c_ctx, w_mod, b_mod, norm_g, ffn_w_in, ffn_w_out, ev_w_in, ev_w_out, ssd_conv_w, ssd_conv_b,
           ssd_dt_bias, ssd_a_log, ssd_d, ssd_norm_g, cm_conv_w, cm_conv_b, cm_ln_g, cm_ln_b, od_w_in, od_w_out,
           na_q_g, na_k_g, na_rpb, sc_conv_w):
    bsz, seq, d = x.shape
    n_ctx = ctx.shape[1]
    rows = seq // GRID_W

    cvec = jnp.concatenate([c, c_ctx[None, :], jnp.zeros((8 - bsz - 1, d), F32)], axis=0)
    mod = modulation(cvec, w_mod, b_mod).reshape(DEPTH, 8, N_MOD, d)

    ffn_in = ffn_w_in.astype(BF16)
    ffn_out = ffn_w_out.astype(BF16)
    ev_out = ev_w_out.astype(BF16)
    od_out = od_w_out.astype(BF16)

    xs = x.reshape(bsz * seq, d)
    cs = ctx.reshape(bsz * n_ctx, d)
    for i in range(DEPTH):
        last = i == DEPTH - 1
        sx = _Stream(mod[i, :bsz], seq, 1024)
        sc = _Stream(mod[i, bsz:bsz + 1], bsz * n_ctx, bsz * n_ctx)
        g = norm_g[i]
        xs = _ffn_half(xs, sx, g[0:1], ffn_in[i, 0], ffn_out[i, 0], 0)
        cs = _ffn_half(cs, sc, g[0:1], ffn_in[i, 0], ffn_out[i, 0], 0)
        j = i // 2
        if i % 2 == 0:
            names = ("z", "xbc", "dt", "ga", "gg")
            sizes = (SSD_WIDTH, SSD_XBC, 2 * SSD_HEADS, CM_WIDTH, CM_WIDTH)
            a_neg = -jnp.exp(ssd_a_log[j].astype(F32))
            p = {
                "w_in": dict(zip(names, _split_cols(ev_w_in[j], sizes))),
                "dt_bias": ssd_dt_bias[j].reshape(1, 2 * SSD_HEADS).astype(F32),
                "conv_w": ssd_conv_w[j], "conv_b": ssd_conv_b[j][None, :],
                "a_row": a_neg[:, None, :], "a_col": a_neg[:, :, None],
                "cm_w": cm_conv_w[j], "cm_b": cm_conv_b[j][None, :],
                "dvec": jnp.repeat(ssd_d[j], SSD_HEAD_DIM)[None, :],
                "norm_g": ssd_norm_g[j][None, :], "ln_g": cm_ln_g[j][None, :], "ln_b": cm_ln_b[j][None, :],
            }
            lhs_x, lhs_c = _mixer_even(xs, cs, sx, sc, g[1:2], p, bsz, seq, n_ctx)
            w_o = ev_out[j]
            xs = matmul_residual([lhs_x], [w_o], [0], xs, sx.m(5), 1.0,
                                 rows_per_mod=sx.rows_per_mod, tm=sx.tm, tn=512, name="mixer_out")
            cs_mix = (lhs_c,), (w_o,), (0,)
        else:
            names = ("q", "k", "v", "gb", "gc", "hs")
            p = {
                "w_in": dict(zip(names, _split_cols(od_w_in[j], (NA_WIDTH,) * 3 + (SC_WIDTH,) * 3))),
                "q_g": na_q_g[j][None, :], "k_g": na_k_g[j][None, :],
                "bias": _na_bias_tables(na_rpb[j], rows), "sc_w": sc_conv_w[j],
            }
            lx, lc = _mixer_odd(xs, cs, sx, sc, g[1:2], p, bsz, seq, n_ctx, not last)
            w_o = (od_out[j][:NA_WIDTH], od_out[j][NA_WIDTH:])
            xs = matmul_residual(list(lx), list(w_o), [0, 0], xs, sx.m(5), 1.0,
                                 rows_per_mod=sx.rows_per_mod, tm=sx.tm, tn=512, name="mixer_out")
            cs_mix = (lc, w_o, (0, 0)) if lc is not None else None
        xs = _ffn_half(xs, sx, g[2:3], ffn_in[i, 1], ffn_out[i, 1], 2)
        if not last:
            cs = matmul_residual(list(cs_mix[0]), list(cs_mix[1]), list(cs_mix[2]), cs, sc.m(5), 1.0,
                                 rows_per_mod=sc.rows_per_mod, tm=sc.tm, tn=512, name="mixer_out")
            cs = _ffn_half(cs, sc, g[2:3], ffn_in[i, 1], ffn_out[i, 1], 2)
    return xs.reshape(bsz, seq, d)
```

```python
import functools

import numpy as np
import jax
import jax.numpy as jnp
from jax import lax
from jax.experimental import pallas as pl
from jax.experimental.pallas import tpu as pltpu

F32 = jnp.float32
BF16 = jnp.bfloat16

D_MODEL = 2048
DEPTH = 4
GRID_W = 64
N_MOD = 9
FFN_DIM = 5632
SSD_HEADS = 32
SSD_HEAD_DIM = 64
SSD_WIDTH = SSD_HEADS * SSD_HEAD_DIM
SSD_GROUPS = 8
SSD_STATE = 128
SSD_CHUNK = 128
SSD_GN = SSD_GROUPS * SSD_STATE
SSD_XBC = SSD_WIDTH + 2 * SSD_GN
SSD_HPG = SSD_HEADS // SSD_GROUPS
SSD_GW = SSD_HPG * SSD_HEAD_DIM
CM_WIDTH = 2048
CM_CONV = 31
NA_HEADS = 16
NA_HEAD_DIM = 128
NA_WIDTH = NA_HEADS * NA_HEAD_DIM
NA_ROWS = 8
NA_COLS = 16
SC_WIDTH = 2048
EPS = 1e-6

NA_QROWS = 8
NA_KROWS = 16
NEG_BIG = -1e30

VMEM_LIMIT = 56 * 1024 * 1024


def _cparams(sem):
    return pltpu.CompilerParams(dimension_semantics=sem, vmem_limit_bytes=VMEM_LIMIT)


def _silu(v):
    return v * jax.nn.sigmoid(v)


def _mod_kernel(c_ref, w_ref, b_ref, o_ref):
    s = _silu(c_ref[...]).astype(BF16)
    o_ref[0] = jnp.dot(s, w_ref[0].astype(BF16), preferred_element_type=F32) + b_ref[0]


def modulation(cvec, w_mod, b_mod, tn=1024):
    depth, d, n = w_mod.shape
    rows = cvec.shape[0]
    return pl.pallas_call(
        _mod_kernel,
        grid=(depth, n // tn),
        in_specs=[pl.BlockSpec((rows, d), lambda l, j: (0, 0)),
                  pl.BlockSpec((1, d, tn), lambda l, j: (l, 0, j)),
                  pl.BlockSpec((1, 1, tn), lambda l, j: (l, 0, j))],
        out_specs=pl.BlockSpec((1, rows, tn), lambda l, j: (l, 0, j)),
        out_shape=jax.ShapeDtypeStruct((depth, rows, n), F32),
        compiler_params=_cparams(("parallel", "parallel")),
        name="modulation",
    )(cvec, w_mod, b_mod.reshape(depth, 1, n))


NORM_ROWS = 16
NORM_UNROLL = 4


def _nm_kernel(n_w, n_extra, epilogue, tm, x_ref, sh_ref, sc_ref, g_ref, *rest):
    w_refs = rest[:n_w]
    extra = rest[n_w:n_w + n_extra]
    o_ref = rest[n_w + n_extra]
    xn_ref = rest[n_w + n_extra + 1]

    @pl.when(pl.program_id(1) == 0)
    def _():
        def body(r, carry):
            rows = pl.ds(pl.multiple_of(r * NORM_ROWS, NORM_ROWS), NORM_ROWS)
            xv = x_ref[rows, :]
            ms = jnp.mean(xv * xv, axis=-1, keepdims=True)
            y = xv * lax.rsqrt(ms + EPS) * g_ref[...]
            y = y * (1.0 + sc_ref[0]) + sh_ref[0]
            xn_ref[rows, :] = y.astype(BF16)
            return carry
        lax.fori_loop(0, tm // NORM_ROWS, body, 0, unroll=NORM_UNROLL)

    xn = xn_ref[...]
    accs = [jnp.dot(xn, w[...].astype(BF16), preferred_element_type=F32) for w in w_refs]
    epilogue(accs, extra, o_ref)


def norm_matmul(x, shift, scale, g, ws, col_offs, n_out, epilogue, out_dtype, *, rows_per_mod, tm, tn,
                extras=(), name="norm_matmul"):
    t, d = x.shape
    assert t % tm == 0 and n_out % tn == 0 and rows_per_mod % tm == 0
    for off in col_offs:
        assert off % tn == 0
    mod_spec = pl.BlockSpec((1, 1, d), lambda i, j: ((i * tm) // rows_per_mod, 0, 0))
    in_specs = [pl.BlockSpec((tm, d), lambda i, j: (i, 0)), mod_spec, mod_spec,
                pl.BlockSpec((1, d), lambda i, j: (0, 0))]
    for (_, prefix), off in zip(ws, col_offs):
        in_specs.append(pl.BlockSpec((None,) * len(prefix) + (d, tn),
                                     functools.partial(lambda i, j, p, o: p + (0, j + o), p=prefix, o=off // tn)))
    ws = [w for w, _ in ws]
    for e in extras:
        if e.shape[1] == n_out:
            in_specs.append(pl.BlockSpec((1, tn), lambda i, j: (0, j)))
        else:
            in_specs.append(pl.BlockSpec(e.shape, lambda i, j: (0, 0)))
    kern = functools.partial(_nm_kernel, len(ws), len(extras), epilogue, tm)
    return pl.pallas_call(
        kern,
        grid=(t // tm, n_out // tn),
        in_specs=in_specs,
        out_specs=pl.BlockSpec((tm, tn), lambda i, j: (i, j)),
        out_shape=jax.ShapeDtypeStruct((t, n_out), out_dtype),
        scratch_shapes=[pltpu.VMEM((tm, d), BF16)],
        compiler_params=_cparams(("parallel", "arbitrary")),
        name=name,
    )(x, shift, scale, g, *ws, *extras)


def _ep_swiglu(accs, extra, o_ref):
    a, g = accs
    o_ref[...] = (_silu(a) * g).astype(o_ref.dtype)


def _ep_glu_sigmoid(accs, extra, o_ref):
    a, g = accs
    o_ref[...] = (a * jax.nn.sigmoid(g)).astype(o_ref.dtype)


def _ep_mul(accs, extra, o_ref):
    a, g = accs
    o_ref[...] = (a * g).astype(o_ref.dtype)


def _ep_silu(accs, extra, o_ref):
    o_ref[...] = _silu(accs[0]).astype(o_ref.dtype)


def _ep_id(accs, extra, o_ref):
    o_ref[...] = accs[0].astype(o_ref.dtype)


def _ep_softplus_bias(accs, extra, o_ref):
    v = accs[0] + extra[0][...]
    o_ref[...] = jnp.maximum(v, 0.0) + jnp.log1p(jnp.exp(-jnp.abs(v)))


def _ep_head_rms(post_scale, accs, extra, o_ref):
    a = accs[0]
    hg = extra[0][...]
    for h in range(a.shape[1] // NA_HEAD_DIM):
        sl = slice(h * NA_HEAD_DIM, (h + 1) * NA_HEAD_DIM)
        ah = a[:, sl]
        ms = jnp.mean(ah * ah, axis=-1, keepdims=True)
        y = ah * lax.rsqrt(ms + EPS) * hg
        if post_scale is not None:
            y = y * post_scale
        o_ref[:, sl] = y.astype(o_ref.dtype)


def _mr_kernel(n_p, coef, *refs):
    lhs = refs[:n_p]
    ws = refs[n_p:2 * n_p]
    x_ref, gate_ref, o_ref = refs[2 * n_p:]
    acc = jnp.dot(lhs[0][...], ws[0][...].astype(BF16), preferred_element_type=F32)
    for p in range(1, n_p):
        acc = acc + jnp.dot(lhs[p][...], ws[p][...].astype(BF16), preferred_element_type=F32)
    gate = gate_ref[0]
    if coef != 1.0:
        gate = coef * gate
    o_ref[...] = x_ref[...] + gate * acc


def matmul_residual(lhs_list, w_list, lhs_offs, x, gate, coef, *, rows_per_mod, tm, tn, name="matmul_residual"):
    t, d = x.shape
    assert t % tm == 0 and d % tn == 0 and rows_per_mod % tm == 0
    in_specs = []
    for l, (_, _, _, kp), off in zip(lhs_list, w_list, lhs_offs):
        assert off % kp == 0
        in_specs.append(pl.BlockSpec((tm, kp), functools.partial(lambda i, j, o: (i, o), o=off // kp)))
    for _, prefix, rblk, kp in w_list:
        in_specs.append(pl.BlockSpec((None,) * len(prefix) + (kp, tn),
                                     functools.partial(lambda i, j, p, r: p + (r, j), p=prefix, r=rblk)))
    w_list = [w for w, _, _, _ in w_list]
    in_specs.append(pl.BlockSpec((tm, tn), lambda i, j: (i, j)))
    in_specs.append(pl.BlockSpec((1, 1, tn), lambda i, j: ((i * tm) // rows_per_mod, 0, j)))
    return pl.pallas_call(
        functools.partial(_mr_kernel, len(lhs_list), coef),
        grid=(t // tm, d // tn),
        in_specs=in_specs,
        out_specs=pl.BlockSpec((tm, tn), lambda i, j: (i, j)),
        out_shape=jax.ShapeDtypeStruct((t, d), F32),
        compiler_params=_cparams(("parallel", "parallel")),
        name=name,
    )(*lhs_list, *w_list, x, gate)


def _shift_rows(v, s):
    n = v.shape[0]
    r = pltpu.roll(v, s % n, axis=0)
    row = lax.broadcasted_iota(jnp.int32, v.shape, 0)
    if s > 0:
        return jnp.where(row < s, 0.0, r)
    return jnp.where(row >= n + s, 0.0, r)


def _conv3(v, w_ref):
    return (w_ref[0:1, :] * _shift_rows(v, 1) + w_ref[1:2, :] * v) + w_ref[2:3, :] * _shift_rows(v, -1)


def _conv3_silu_kernel(n_c, xc_ref, xx_ref, w_ref, b_ref, o_ref):
    for src, lo, ln in ((xc_ref, 0, n_c), (xx_ref, n_c, xx_ref.shape[1])):
        y = _conv3(src[0], w_ref) + b_ref[...]
        o_ref[0, lo:lo + ln, :] = _silu(y)


def conv3_silu_joint(xc, xx, w, b, ct=256):
    bsz, lc, ch = xc.shape
    lx = xx.shape[1]
    return pl.pallas_call(
        functools.partial(_conv3_silu_kernel, lc),
        grid=(bsz, ch // ct),
        in_specs=[pl.BlockSpec((1, lc, ct), lambda b_, j: (b_, 0, j)),
                  pl.BlockSpec((1, lx, ct), lambda b_, j: (b_, 0, j)),
                  pl.BlockSpec((3, ct), lambda b_, j: (0, j)),
                  pl.BlockSpec((1, ct), lambda b_, j: (0, j))],
        out_specs=pl.BlockSpec((1, lc + lx, ct), lambda b_, j: (b_, 0, j)),
        out_shape=jax.ShapeDtypeStruct((bsz, lc + lx, ch), F32),
        compiler_params=_cparams(("parallel", "parallel")),
        name="ssd_conv3_silu",
    )(xc, xx, w, b)


def _gated_conv3_kernel(p_ref, gate_ref, w_ref, o_ref):
    o_ref[0] = (gate_ref[0] * _conv3(p_ref[0], w_ref)).astype(o_ref.dtype)


def gated_conv3(prod, gate, w, ct=256):
    bsz, ln, ch = prod.shape
    spec = pl.BlockSpec((1, ln, ct), lambda b_, j: (b_, 0, j))
    return pl.pallas_call(
        _gated_conv3_kernel,
        grid=(bsz, ch // ct),
        in_specs=[spec, spec, pl.BlockSpec((3, ct), lambda b_, j: (0, j))],
        out_specs=spec,
        out_shape=jax.ShapeDtypeStruct((bsz, ln, ch), BF16),
        compiler_params=_cparams(("parallel", "parallel")),
        name="short_gated_conv",
    )(prod, gate, w)


CONV_HALO = 16
CONV_ROWS = 64
SUBLANES = 8
LANES = 128


def _conv31_kernel(n_c, uc_ref, ux_ref, w_ref, b_ref, o_ref, sh_ref):
    ct = o_ref.shape[2]

    def run(src_ref, lo, ln):
        lp = ln + 2 * CONV_HALO
        zeros = jnp.zeros((CONV_HALO, ct), F32)
        sh_ref[0, 0:CONV_HALO, :] = zeros
        sh_ref[0, CONV_HALO:CONV_HALO + ln, :] = src_ref[0]
        sh_ref[0, CONV_HALO + ln:lp, :] = zeros
        p = sh_ref[0, 0:lp, :]
        for s in range(1, SUBLANES):
            sh_ref[s, 0:lp, :] = pltpu.roll(p, lp - s, axis=0)

        def body(ci, carry):
            r0 = pl.multiple_of(ci * CONV_ROWS, CONV_ROWS)
            acc = jnp.zeros((CONV_ROWS, ct), F32) + b_ref[...]
            for k in range(CM_CONV):
                a, s = divmod(k + CONV_HALO - (CM_CONV - 1) // 2, SUBLANES)
                start = pl.multiple_of(r0 + SUBLANES * a, SUBLANES)
                acc = acc + sh_ref[s, pl.ds(start, CONV_ROWS), :] * w_ref[k:k + 1, :]
            o_ref[0, pl.ds(pl.multiple_of(lo + r0, CONV_ROWS), CONV_ROWS), :] = acc
            return carry
        lax.fori_loop(0, ln // CONV_ROWS, body, 0)

    run(uc_ref, 0, n_c)
    run(ux_ref, n_c, ux_ref.shape[1])


def conv31_joint(uc, ux, w, b, ct=128):
    bsz, lc, ch = uc.shape
    lx = ux.shape[1]
    return pl.pallas_call(
        functools.partial(_conv31_kernel, lc),
        grid=(bsz, ch // ct),
        in_specs=[pl.BlockSpec((1, lc, ct), lambda b_, j: (b_, 0, j)),
                  pl.BlockSpec((1, lx, ct), lambda b_, j: (b_, 0, j)),
                  pl.BlockSpec((CM_CONV, ct), lambda b_, j: (0, j)),
                  pl.BlockSpec((1, ct), lambda b_, j: (0, j))],
        out_specs=pl.BlockSpec((1, lc + lx, ct), lambda b_, j: (b_, 0, j)),
        out_shape=jax.ShapeDtypeStruct((bsz, lc + lx, ch), F32),
        scratch_shapes=[pltpu.VMEM((SUBLANES, lx + 2 * CONV_HALO, ct), F32)],
        compiler_params=_cparams(("parallel", "parallel")),
        name="conformer_conv31",
    )(uc, ux, w, b)


def _ssd_kernel(xs_ref, b_ref, c_ref, dt_ref, dtt_ref, ar_ref, ac_ref, y_ref, state_ref):
    q = SSD_CHUNK
    sgn = 1 - 2 * pl.program_id(0)

    @pl.when(pl.program_id(2) == 0)
    def _():
        state_ref[...] = jnp.zeros(state_ref.shape, F32)

    dtc = dt_ref[0, 0]
    dtr = dtt_ref[0, 0]
    da_c = dtc * ar_ref[0]
    da_r = dtr * ac_ref[0]
    ri = lax.broadcasted_iota(jnp.int32, (q, q), 0)
    ci = lax.broadcasted_iota(jnp.int32, (q, q), 1)
    keep = (ci - ri) * sgn <= 0
    tri_c = jnp.where(keep, 1.0, 0.0)
    tri_r = jnp.where((ri - ci) * sgn <= 0, 1.0, 0.0)
    acum_c = jnp.dot(tri_c, da_c, preferred_element_type=F32, precision=lax.Precision.HIGHEST)
    acum_r = jnp.dot(da_r, tri_r, preferred_element_type=F32, precision=lax.Precision.HIGHEST)
    alast_c = jnp.sum(da_c, axis=0, keepdims=True)
    lane_head = lax.broadcasted_iota(jnp.int32, (q, SSD_GW), 1) // SSD_HEAD_DIM

    def per_head(cols):
        out = cols[-1]
        for r in reversed(range(SSD_HPG - 1)):
            out = jnp.where(lane_head == r, cols[r], out)
        return out

    for g in range(SSD_GROUPS):
        bg = b_ref[0, :, g * SSD_STATE:(g + 1) * SSD_STATE].astype(BF16)
        cg = c_ref[0, :, g * SSD_STATE:(g + 1) * SSD_STATE].astype(BF16)
        cb = lax.dot_general(cg, bg, (((1,), (1,)), ((), ())), preferred_element_type=F32)
        h_prev = state_ref[g]
        y_off = jnp.dot(cg, h_prev.astype(BF16), preferred_element_type=F32)
        gcol = slice(g * SSD_GW, (g + 1) * SSD_GW)
        xs_g = xs_ref[0, :, gcol]
        xs_b = xs_g.astype(BF16)
        y_d, e_ac, w_st, dec = [], [], [], []
        for r in range(SSD_HPG):
            h = g * SSD_HPG + r
            ac = acum_c[:, h:h + 1]
            ar = acum_r[h:h + 1, :]
            decay = jnp.exp(jnp.where(keep, ac - ar, NEG_BIG))
            mix = (cb * decay * dtr[h:h + 1, :]).astype(BF16)
            y_d.append(jnp.dot(mix, xs_b, preferred_element_type=F32))
            al = alast_c[:, h:h + 1]
            e_ac.append(jnp.exp(ac))
            w_st.append(jnp.exp(al - ac) * dtc[:, h:h + 1])
            dec.append(jnp.exp(al))
        y_ref[0, 0, :, gcol] = per_head(y_d) + y_off * per_head(e_ac)
        xw = (xs_g * per_head(w_st)).astype(BF16)
        s_new = lax.dot_general(bg, xw, (((0,), (0,)), ((), ())), preferred_element_type=F32)
        state_ref[g] = h_prev * per_head(dec) + s_new


def ssd_scan_joint(xbc, dt, dtt, a_row, a_col, n_ctx_chunks):
    bsz, ln, _ = xbc.shape
    nc = ln // SSD_CHUNK
    q = SSD_CHUNK

    def pos(d, c):
        back = jnp.where(c < n_ctx_chunks, n_ctx_chunks - 1 - c, nc + n_ctx_chunks - 1 - c)
        return jnp.where(d == 0, c, back)

    return pl.pallas_call(
        _ssd_kernel,
        grid=(2, bsz, nc),
        in_specs=[pl.BlockSpec((1, q, SSD_WIDTH), lambda d, b, c: (b, pos(d, c), 0)),
                  pl.BlockSpec((1, q, SSD_GN), lambda d, b, c: (b, pos(d, c), SSD_WIDTH // SSD_GN)),
                  pl.BlockSpec((1, q, SSD_GN), lambda d, b, c: (b, pos(d, c), SSD_WIDTH // SSD_GN + 1)),
                  pl.BlockSpec((1, 1, q, SSD_HEADS), lambda d, b, c: (d, b, pos(d, c), 0)),
                  pl.BlockSpec((1, 1, SSD_HEADS, q), lambda d, b, c: (d, b, 0, pos(d, c))),
                  pl.BlockSpec((1, 1, SSD_HEADS), lambda d, b, c: (d, 0, 0)),
                  pl.BlockSpec((1, SSD_HEADS, 1), lambda d, b, c: (d, 0, 0))],
        out_specs=pl.BlockSpec((1, 1, q, SSD_WIDTH), lambda d, b, c: (d, b, pos(d, c), 0)),
        out_shape=jax.ShapeDtypeStruct((2, bsz, ln, SSD_WIDTH), F32),
        scratch_shapes=[pltpu.VMEM((SSD_GROUPS, SSD_STATE, SSD_GW), F32)],
        compiler_params=_cparams(("parallel", "parallel", "arbitrary")),
        name="ssd_scan",
    )(xbc, xbc, xbc, dt, dtt, a_row, a_col)


EO_ROWS = 16


def _even_out_kernel(tq, yf_ref, yb_ref, xs_ref, zs_ref, cu_ref, dv_ref, ng_ref, lg_ref, lb_ref, o_ref):
    gw = SSD_WIDTH // SSD_GROUPS

    def body(r, carry):
        rows = pl.ds(pl.multiple_of(r * EO_ROWS, EO_ROWS), EO_ROWS)
        y = yf_ref[0, 0, rows, :] + yb_ref[0, 0, rows, :]
        y = y + dv_ref[...] * xs_ref[0, rows, :]
        y = y * zs_ref[rows, :]
        for g in range(SSD_GROUPS):
            sl = slice(g * gw, (g + 1) * gw)
            yg = y[:, sl]
            ms = jnp.mean(yg * yg, axis=-1, keepdims=True)
            o_ref[rows, sl] = (yg * lax.rsqrt(ms + EPS) * ng_ref[:, sl]).astype(o_ref.dtype)
        cv = cu_ref[0, rows, :]
        mu = jnp.mean(cv, axis=-1, keepdims=True)
        xc = cv - mu
        var = jnp.mean(xc * xc, axis=-1, keepdims=True)
        ln = xc * lax.rsqrt(var + EPS) * lg_ref[...] + lb_ref[...]
        o_ref[rows, SSD_WIDTH:SSD_WIDTH + CM_WIDTH] = _silu(ln).astype(o_ref.dtype)
        return carry
    lax.fori_loop(0, tq // EO_ROWS, body, 0)


def even_out(y, xbc, zs, cu, dvec, ng, lg, lb, *, blk0, nblk, tq):
    bsz = xbc.shape[0]
    vec = pl.BlockSpec((1, SSD_WIDTH), lambda b, i: (0, 0))
    return pl.pallas_call(
        functools.partial(_even_out_kernel, tq),
        grid=(bsz, nblk),
        in_specs=[pl.BlockSpec((1, 1, tq, SSD_WIDTH), lambda b, i: (0, b, blk0 + i, 0)),
                  pl.BlockSpec((1, 1, tq, SSD_WIDTH), lambda b, i: (1, b, blk0 + i, 0)),
                  pl.BlockSpec((1, tq, SSD_WIDTH), lambda b, i: (b, blk0 + i, 0)),
                  pl.BlockSpec((tq, SSD_WIDTH), lambda b, i: (b * nblk + i, 0)),
                  pl.BlockSpec((1, tq, CM_WIDTH), lambda b, i: (b, blk0 + i, 0)),
                  vec, vec, vec, vec],
        out_specs=pl.BlockSpec((tq, SSD_WIDTH + CM_WIDTH), lambda b, i: (b * nblk + i, 0)),
        out_shape=jax.ShapeDtypeStruct((bsz * nblk * tq, SSD_WIDTH + CM_WIDTH), BF16),
        compiler_params=_cparams(("parallel", "parallel")),
        name="even_mixer_tail",
    )(y, y, xbc, zs, cu, dvec, ng, lg, lb)


_NT = (((1,), (1,)), ((), ()))


def _na_kernel(rows_total, q_ref, k_ref, v_ref, kc_ref, vc_ref, bias_ref, o_ref):
    rb = pl.program_id(2)
    k0 = jnp.clip(rb * NA_QROWS - NA_ROWS // 2, 0, rows_total - NA_KROWS) * GRID_W
    k0 = pl.multiple_of(k0, GRID_W)
    nk = NA_KROWS * GRID_W
    qv = q_ref[...]
    s = lax.dot_general(qv, k_ref[pl.ds(k0, nk), :], _NT, preferred_element_type=F32) + bias_ref[0, 0]
    sc = lax.dot_general(qv, kc_ref[...], _NT, preferred_element_type=F32)
    m = jnp.maximum(jnp.max(s, axis=-1, keepdims=True), jnp.max(sc, axis=-1, keepdims=True))
    p = jnp.exp(s - m)
    pc = jnp.exp(sc - m)
    den = jnp.sum(p, axis=-1, keepdims=True) + jnp.sum(pc, axis=-1, keepdims=True)
    o = jnp.dot(p.astype(BF16), v_ref[pl.ds(k0, nk), :], preferred_element_type=F32)
    o = o + jnp.dot(pc.astype(BF16), vc_ref[...], preferred_element_type=F32)
    o_ref[...] = (o / den).astype(o_ref.dtype)


def neighbourhood_attention(q, k, v, kc, vc, bias, bsz, seq, n_ctx):
    rows = seq // GRID_W
    nrb = rows // NA_QROWS
    tq = NA_QROWS * GRID_W
    nk = NA_KROWS * GRID_W

    def btype(rb):
        return jnp.where(rb == 0, 0, jnp.where(rb == nrb - 1, 2, 1))

    return pl.pallas_call(
        functools.partial(_na_kernel, rows),
        grid=(NA_HEADS, bsz, nrb),
        in_specs=[pl.BlockSpec((tq, NA_HEAD_DIM), lambda h, b, r: (b * nrb + r, h)),
                  pl.BlockSpec((seq, NA_HEAD_DIM), lambda h, b, r: (b, h)),
                  pl.BlockSpec((seq, NA_HEAD_DIM), lambda h, b, r: (b, h)),
                  pl.BlockSpec((n_ctx, NA_HEAD_DIM), lambda h, b, r: (b, h)),
                  pl.BlockSpec((n_ctx, NA_HEAD_DIM), lambda h, b, r: (b, h)),
                  pl.BlockSpec((1, 1, tq, nk), lambda h, b, r: (btype(r), h, 0, 0))],
        out_specs=pl.BlockSpec((tq, NA_HEAD_DIM), lambda h, b, r: (b * nrb + r, h)),
        out_shape=jax.ShapeDtypeStruct((bsz * seq, NA_WIDTH), BF16),
        compiler_params=_cparams(("parallel", "parallel", "arbitrary")),
        name="neighbourhood_attention",
    )(q, k, v, kc, vc, bias)


def _ctx_attn_kernel(q_ref, k_ref, v_ref, o_ref):
    s = lax.dot_general(q_ref[...], k_ref[...], _NT, preferred_element_type=F32)
    m = jnp.max(s, axis=-1, keepdims=True)
    p = jnp.exp(s - m)
    den = jnp.sum(p, axis=-1, keepdims=True)
    o = jnp.dot(p.astype(BF16), v_ref[...], preferred_element_type=F32)
    o_ref[...] = (o / den).astype(o_ref.dtype)


def context_attention(q, k, v, bsz, n_ctx):
    spec = pl.BlockSpec((n_ctx, NA_HEAD_DIM), lambda b, h: (b, h))
    return pl.pallas_call(
        _ctx_attn_kernel,
        grid=(bsz, NA_HEADS),
        in_specs=[spec, spec, spec],
        out_specs=spec,
        out_shape=jax.ShapeDtypeStruct((bsz * n_ctx, NA_WIDTH), BF16),
        compiler_params=_cparams(("parallel", "parallel")),
        name="context_attention",
    )(q, k, v)


def _na_bias_tables(rpb, rows):
    nrb = rows // NA_QROWS
    col = np.arange(GRID_W)
    c0 = np.clip(col - NA_COLS // 2, 0, GRID_W - NA_COLS)
    col_ok = (col[None, :] >= c0[:, None]) & (col[None, :] < c0[:, None] + NA_COLS)
    col_idx = np.clip(col[None, :] - col[:, None] + NA_COLS - 1, 0, 2 * NA_COLS - 2)
    kr = min(NA_ROWS, rows)
    ridx = np.zeros((3, NA_QROWS, NA_KROWS), np.int32)
    rok = np.zeros((3, NA_QROWS, NA_KROWS), bool)
    for t, rb in enumerate((0, 1, nrb - 1)):
        k0 = int(np.clip(rb * NA_QROWS - NA_ROWS // 2, 0, rows - NA_KROWS))
        for rq in range(NA_QROWS):
            r = rb * NA_QROWS + rq
            r0 = int(np.clip(r - kr // 2, 0, rows - kr))
            for rk in range(NA_KROWS):
                ra = k0 + rk
                rok[t, rq, rk] = r0 <= ra < r0 + kr
                ridx[t, rq, rk] = int(np.clip(ra - r + NA_ROWS - 1, 0, 2 * NA_ROWS - 2))
    tz = rpb[:, :, col_idx].astype(F32)
    tz = jnp.where(col_ok[None, None], tz, NEG_BIG)
    masked = jnp.full((NA_HEADS, GRID_W, GRID_W), NEG_BIG, F32)
    slabs = [jnp.concatenate([tz[:, ridx[t, rq, rk]] if rok[t, rq, rk] else masked for rk in range(NA_KROWS)],
                             axis=-1)
             for t in range(3) for rq in range(NA_QROWS)]
    big = jnp.stack(slabs, axis=0).reshape(3, NA_QROWS, NA_HEADS, GRID_W, NA_KROWS * GRID_W)
    big = jnp.transpose(big, (0, 2, 1, 3, 4))
    return big.reshape(3, NA_HEADS, NA_QROWS * GRID_W, NA_KROWS * GRID_W)


class _Stream:
    def __init__(self, mod, rows_per_mod, tm):
        self.mod = mod
        self.rows_per_mod = rows_per_mod
        self.tm = tm

    def m(self, idx):
        return self.mod[:, idx][:, None, :]


def _ffn_half(h, st, g, w_in, w_out, j):
    hid = norm_matmul(h, st.m(3 * j), st.m(3 * j + 1), g, [w_in, w_in], [0, FFN_DIM], FFN_DIM, _ep_swiglu, BF16,
                      rows_per_mod=st.rows_per_mod, tm=st.tm, tn=512, name="ffn_in")
    return matmul_residual([hid], [w_out + (0, FFN_DIM)], [0], h, st.m(3 * j + 2), 0.5,
                           rows_per_mod=st.rows_per_mod, tm=st.tm, tn=256, name="ffn_out")


def _proj(h, st, g, w, names, n_out, ep, dtype, tn=512, extras=(), name="mixer_in"):
    return norm_matmul(h, st.m(3), st.m(4), g, [w[n][0] for n in names], [w[n][1] for n in names], n_out, ep, dtype,
                       rows_per_mod=st.rows_per_mod, tm=st.tm, tn=tn, extras=extras, name=name)


def _mixer_even(x, cx, sx, sc, g, p, bsz, seq, n_ctx):
    w = p["w_in"]
    outs = {}
    for key, h, st in (("x", x, sx), ("c", cx, sc)):
        zs = _proj(h, st, g, w, ("z",), SSD_WIDTH, _ep_silu, F32, name="ssd_z")
        xbc = _proj(h, st, g, w, ("xbc",), SSD_XBC, _ep_id, F32, name="ssd_xbc")
        dt = _proj(h, st, g, w, ("dt",), 2 * SSD_HEADS, _ep_softplus_bias, F32, tn=2 * SSD_HEADS,
                   extras=(p["dt_bias"],), name="ssd_dt")
        u = _proj(h, st, g, w, ("ga", "gg"), CM_WIDTH, _ep_glu_sigmoid, F32, name="conformer_glu")
        outs[key] = (zs, xbc, dt, u)
    zs_x, xbc_x, dt_x, u_x = outs["x"]
    zs_c, xbc_c, dt_c, u_c = outs["c"]
    xbc = conv3_silu_joint(xbc_c.reshape(bsz, n_ctx, SSD_XBC), xbc_x.reshape(bsz, seq, SSD_XBC),
                           p["conv_w"], p["conv_b"])
    dt = jnp.concatenate([dt_c.reshape(bsz, n_ctx, 2, SSD_HEADS), dt_x.reshape(bsz, seq, 2, SSD_HEADS)], axis=1)
    dt = jnp.transpose(dt, (2, 0, 1, 3))
    dtt = jnp.transpose(dt, (0, 1, 3, 2))
    y = ssd_scan_joint(xbc, dt, dtt, p["a_row"], p["a_col"], n_ctx // SSD_CHUNK)
    cu = conv31_joint(u_c.reshape(bsz, n_ctx, CM_WIDTH), u_x.reshape(bsz, seq, CM_WIDTH), p["cm_w"], p["cm_b"])
    tq = n_ctx
    tail = functools.partial(even_out, y, xbc, dvec=p["dvec"], ng=p["norm_g"], lg=p["ln_g"], lb=p["ln_b"], tq=tq)
    lhs_x = tail(zs=zs_x, cu=cu, blk0=1, nblk=seq // tq)
    lhs_c = tail(zs=zs_c, cu=cu, blk0=0, nblk=1)
    return lhs_x, lhs_c


def _mixer_odd(x, cx, sx, sc, g, p, bsz, seq, n_ctx, need_ctx):
    w = p["w_in"]
    hg_q, hg_k = p["q_g"], p["k_g"]
    ep_q = functools.partial(_ep_head_rms, NA_HEAD_DIM ** -0.5)
    ep_k = functools.partial(_ep_head_rms, None)
    kc = _proj(cx, sc, g, w, ("k",), NA_WIDTH, ep_k, BF16, extras=(hg_k,), name="na_k")
    vc = _proj(cx, sc, g, w, ("v",), NA_WIDTH, _ep_id, BF16, name="na_v")
    q = _proj(x, sx, g, w, ("q",), NA_WIDTH, ep_q, BF16, extras=(hg_q,), name="na_q")
    k = _proj(x, sx, g, w, ("k",), NA_WIDTH, ep_k, BF16, extras=(hg_k,), name="na_k")
    v = _proj(x, sx, g, w, ("v",), NA_WIDTH, _ep_id, BF16, name="na_v")
    gb = _proj(x, sx, g, w, ("gb",), SC_WIDTH, _ep_id, F32, name="sc_gate")
    pr = _proj(x, sx, g, w, ("gc", "hs"), SC_WIDTH, _ep_mul, F32, name="sc_prod")
    ox = neighbourhood_attention(q, k, v, kc, vc, p["bias"], bsz, seq, n_ctx)
    yx = gated_conv3(pr.reshape(bsz, seq, SC_WIDTH), gb.reshape(bsz, seq, SC_WIDTH), p["sc_w"])
    yx = yx.reshape(bsz * seq, SC_WIDTH)
    if not need_ctx:
        return (ox, yx), None
    qc = _proj(cx, sc, g, w, ("q",), NA_WIDTH, ep_q, BF16, extras=(hg_q,), name="na_q")
    gbc = _proj(cx, sc, g, w, ("gb",), SC_WIDTH, _ep_id, F32, name="sc_gate")
    prc = _proj(cx, sc, g, w, ("gc", "hs"), SC_WIDTH, _ep_mul, F32, name="sc_prod")
    oc = context_attention(qc, kc, vc, bsz, n_ctx)
    yc = gated_conv3(prc.reshape(bsz, n_ctx, SC_WIDTH), gbc.reshape(bsz, n_ctx, SC_WIDTH), p["sc_w"])
    yc = yc.reshape(bsz * n_ctx, SC_WIDTH)
    return (ox, yx), (oc, yc)


def _segments(w, prefix, names, sizes, tn=512):
    out, acc = {}, 0
    for n, s in zip(names, sizes):
        if acc % tn == 0 and s % LANES == 0:
            out[n] = ((w, prefix), acc)
        else:
            out[n] = ((w[prefix][:, acc:acc + s], ()), 0)
        acc += s
    return out


def kernel(x, c, ctx, c_ctx, w_mod, b_mod, norm_g, ffn_w_in, ffn_w_out, ev_w_in, ev_w_out, ssd_conv_w, ssd_conv_b,
           ssd_dt_bias, ssd_a_log, ssd_d, ssd_norm_g, cm_conv_w, cm_conv_b, cm_ln_g, cm_ln_b, od_w_in, od_w_out,
           na_q_g, na_k_g, na_rpb, sc_conv_w):
    bsz, seq, d = x.shape
    n_ctx = ctx.shape[1]
    rows = seq // GRID_W

    cvec = jnp.concatenate([c, c_ctx[None, :], jnp.zeros((8 - bsz - 1, d), F32)], axis=0)
    mod = modulation(cvec, w_mod, b_mod).reshape(DEPTH, 8, N_MOD, d)

    xs = x.reshape(bsz * seq, d)
    cs = ctx.reshape(bsz * n_ctx, d)
    for i in range(DEPTH):
        last = i == DEPTH - 1
        sx = _Stream(mod[i, :bsz], seq, 1024)
        sc = _Stream(mod[i, bsz:bsz + 1], bsz * n_ctx, bsz * n_ctx)
        g = norm_g[i]
        xs = _ffn_half(xs, sx, g[0:1], (ffn_w_in, (i, 0)), (ffn_w_out, (i, 0)), 0)
        cs = _ffn_half(cs, sc, g[0:1], (ffn_w_in, (i, 0)), (ffn_w_out, (i, 0)), 0)
        j = i // 2
        if i % 2 == 0:
            names = ("z", "xbc", "dt", "ga", "gg")
            sizes = (SSD_WIDTH, SSD_XBC, 2 * SSD_HEADS, CM_WIDTH, CM_WIDTH)
            a_neg = -jnp.exp(ssd_a_log[j].astype(F32))
            p = {
                "w_in": _segments(ev_w_in, (j,), names, sizes),
                "dt_bias": ssd_dt_bias[j].reshape(1, 2 * SSD_HEADS).astype(F32),
                "conv_w": ssd_conv_w[j], "conv_b": ssd_conv_b[j][None, :],
                "a_row": a_neg[:, None, :], "a_col": a_neg[:, :, None],
                "cm_w": cm_conv_w[j], "cm_b": cm_conv_b[j][None, :],
                "dvec": jnp.repeat(ssd_d[j], SSD_HEAD_DIM)[None, :],
                "norm_g": ssd_norm_g[j][None, :], "ln_g": cm_ln_g[j][None, :], "ln_b": cm_ln_b[j][None, :],
            }
            lhs_x, lhs_c = _mixer_even(xs, cs, sx, sc, g[1:2], p, bsz, seq, n_ctx)
            w_o = [(ev_w_out, (j,), 0, SSD_WIDTH + CM_WIDTH)]
            xs = matmul_residual([lhs_x], w_o, [0], xs, sx.m(5), 1.0,
                                 rows_per_mod=sx.rows_per_mod, tm=sx.tm, tn=256, name="mixer_out")
            cs_mix = [lhs_c], w_o, [0]
        else:
            names = ("q", "k", "v", "gb", "gc", "hs")
            p = {
                "w_in": _segments(od_w_in, (j,), names, (NA_WIDTH,) * 3 + (SC_WIDTH,) * 3),
                "q_g": na_q_g[j][None, :], "k_g": na_k_g[j][None, :],
                "bias": _na_bias_tables(na_rpb[j], rows), "sc_w": sc_conv_w[j],
            }
            lx, lc = _mixer_odd(xs, cs, sx, sc, g[1:2], p, bsz, seq, n_ctx, not last)
            w_o = [(od_w_out, (j,), 0, NA_WIDTH), (od_w_out, (j,), 1, SC_WIDTH)]
            xs = matmul_residual(list(lx), w_o, [0, 0], xs, sx.m(5), 1.0,
                                 rows_per_mod=sx.rows_per_mod, tm=sx.tm, tn=256, name="mixer_out")
            cs_mix = (list(lc), w_o, [0, 0]) if lc is not None else None
        xs = _ffn_half(xs, sx, g[2:3], (ffn_w_in, (i, 1)), (ffn_w_out, (i, 1)), 2)
        if not last:
            cs = matmul_residual(cs_mix[0], cs_mix[1], cs_mix[2], cs, sc.m(5), 1.0,
                                 rows_per_mod=sc.rows_per_mod, tm=sc.tm, tn=256, name="mixer_out")
            cs = _ffn_half(cs, sc, g[2:3], (ffn_w_in, (i, 1)), (ffn_w_out, (i, 1)), 2)
    return xs.reshape(bsz, seq, d)
```

```python
import functools

import numpy as np
import jax
import jax.numpy as jnp
from jax import lax
from jax.experimental import pallas as pl
from jax.experimental.pallas import tpu as pltpu

F32 = jnp.float32
BF16 = jnp.bfloat16

D_MODEL = 2048
DEPTH = 4
GRID_W = 64
N_MOD = 9
FFN_DIM = 5632
SSD_HEADS = 32
SSD_HEAD_DIM = 64
SSD_WIDTH = SSD_HEADS * SSD_HEAD_DIM
SSD_GROUPS = 8
SSD_STATE = 128
SSD_CHUNK = 128
SSD_GN = SSD_GROUPS * SSD_STATE
SSD_XBC = SSD_WIDTH + 2 * SSD_GN
SSD_HPG = SSD_HEADS // SSD_GROUPS
SSD_GW = SSD_HPG * SSD_HEAD_DIM
CM_WIDTH = 2048
CM_CONV = 31
NA_HEADS = 16
NA_HEAD_DIM = 128
NA_WIDTH = NA_HEADS * NA_HEAD_DIM
NA_ROWS = 8
NA_COLS = 16
SC_WIDTH = 2048
EPS = 1e-6

NA_QROWS = 8
NA_KROWS = 16
NA_HPS = 2
NEG_BIG = -1e30

VMEM_LIMIT = 56 * 1024 * 1024


def _cparams(sem):
    return pltpu.CompilerParams(dimension_semantics=sem, vmem_limit_bytes=VMEM_LIMIT)


def _silu(v):
    return v * jax.nn.sigmoid(v)


def _mod_kernel(c_ref, w_ref, b_ref, o_ref):
    s = _silu(c_ref[...]).astype(BF16)
    o_ref[0] = jnp.dot(s, w_ref[0].astype(BF16), preferred_element_type=F32) + b_ref[0]


def modulation(cvec, w_mod, b_mod, tn=1024):
    depth, d, n = w_mod.shape
    rows = cvec.shape[0]
    return pl.pallas_call(
        _mod_kernel,
        grid=(depth, n // tn),
        in_specs=[pl.BlockSpec((rows, d), lambda l, j: (0, 0)),
                  pl.BlockSpec((1, d, tn), lambda l, j: (l, 0, j)),
                  pl.BlockSpec((1, 1, tn), lambda l, j: (l, 0, j))],
        out_specs=pl.BlockSpec((1, rows, tn), lambda l, j: (l, 0, j)),
        out_shape=jax.ShapeDtypeStruct((depth, rows, n), F32),
        compiler_params=_cparams(("parallel", "parallel")),
        name="modulation",
    )(cvec, w_mod, b_mod.reshape(depth, 1, n))


NORM_ROWS = 16
NORM_UNROLL = 4


def _adaln_kernel(tm, x_ref, sh_ref, sc_ref, g_ref, o_ref):
    def body(r, carry):
        rows = pl.ds(pl.multiple_of(r * NORM_ROWS, NORM_ROWS), NORM_ROWS)
        xv = x_ref[rows, :]
        ms = jnp.mean(xv * xv, axis=-1, keepdims=True)
        y = xv * lax.rsqrt(ms + EPS) * g_ref[...]
        y = y * (1.0 + sc_ref[0]) + sh_ref[0]
        o_ref[rows, :] = y.astype(o_ref.dtype)
        return carry
    lax.fori_loop(0, tm // NORM_ROWS, body, 0, unroll=NORM_UNROLL)


def adaln(x, shift, scale, g, *, rows_per_mod, tm=512):
    t, d = x.shape
    assert t % tm == 0 and rows_per_mod % tm == 0
    mod_spec = pl.BlockSpec((1, 1, d), lambda i: ((i * tm) // rows_per_mod, 0, 0))
    return pl.pallas_call(
        functools.partial(_adaln_kernel, tm),
        grid=(t // tm,),
        in_specs=[pl.BlockSpec((tm, d), lambda i: (i, 0)), mod_spec, mod_spec, pl.BlockSpec((1, d), lambda i: (0, 0))],
        out_specs=pl.BlockSpec((tm, d), lambda i: (i, 0)),
        out_shape=jax.ShapeDtypeStruct((t, d), BF16),
        compiler_params=_cparams(("parallel",)),
        name="adaln",
    )(x, shift, scale, g)


def _pm_kernel(n_w, n_extra, epilogue, xn_ref, *rest):
    w_refs = rest[:n_w]
    extra = rest[n_w:n_w + n_extra]
    o_ref = rest[n_w + n_extra]
    wb_refs = rest[n_w + n_extra + 1:]

    @pl.when(pl.program_id(1) == 0)
    def _():
        for w, wb in zip(w_refs, wb_refs):
            wb[...] = w[...].astype(BF16)

    xn = xn_ref[...]
    accs = [jnp.dot(xn, wb[...], preferred_element_type=F32) for wb in wb_refs]
    epilogue(accs, extra, o_ref)


def proj_matmul(xn, ws, col_offs, n_out, epilogue, out_dtype, *, tm, tn, extras=(), name="proj_matmul"):
    t, d = xn.shape
    assert t % tm == 0 and n_out % tn == 0
    for off in col_offs:
        assert off % tn == 0
    in_specs = [pl.BlockSpec((tm, d), lambda j, i: (i, 0))]
    for (_, prefix), off in zip(ws, col_offs):
        in_specs.append(pl.BlockSpec((None,) * len(prefix) + (d, tn),
                                     functools.partial(lambda j, i, p, o: p + (0, j + o), p=prefix, o=off // tn)))
    for e in extras:
        if e.shape[1] == n_out:
            in_specs.append(pl.BlockSpec((1, tn), lambda j, i: (0, j)))
        else:
            in_specs.append(pl.BlockSpec(e.shape, lambda j, i: (0, 0)))
    return pl.pallas_call(
        functools.partial(_pm_kernel, len(ws), len(extras), epilogue),
        grid=(n_out // tn, t // tm),
        in_specs=in_specs,
        out_specs=pl.BlockSpec((tm, tn), lambda j, i: (i, j)),
        out_shape=jax.ShapeDtypeStruct((t, n_out), out_dtype),
        scratch_shapes=[pltpu.VMEM((d, tn), BF16) for _ in ws],
        compiler_params=_cparams(("arbitrary", "arbitrary")),
        name=name,
    )(xn, *[w for w, _ in ws], *extras)


def _ep_swiglu(accs, extra, o_ref):
    a, g = accs
    o_ref[...] = (_silu(a) * g).astype(o_ref.dtype)


def _ep_glu_sigmoid(accs, extra, o_ref):
    a, g = accs
    o_ref[...] = (a * jax.nn.sigmoid(g)).astype(o_ref.dtype)


def _ep_mul(accs, extra, o_ref):
    a, g = accs
    o_ref[...] = (a * g).astype(o_ref.dtype)


def _ep_silu(accs, extra, o_ref):
    o_ref[...] = _silu(accs[0]).astype(o_ref.dtype)


def _ep_id(accs, extra, o_ref):
    o_ref[...] = accs[0].astype(o_ref.dtype)


def _ep_softplus_bias(accs, extra, o_ref):
    v = accs[0] + extra[0][...]
    o_ref[...] = jnp.maximum(v, 0.0) + jnp.log1p(jnp.exp(-jnp.abs(v)))


def _ep_head_rms(post_scale, accs, extra, o_ref):
    a = accs[0]
    hg = extra[0][...]
    for h in range(a.shape[1] // NA_HEAD_DIM):
        sl = slice(h * NA_HEAD_DIM, (h + 1) * NA_HEAD_DIM)
        ah = a[:, sl]
        ms = jnp.mean(ah * ah, axis=-1, keepdims=True)
        y = ah * lax.rsqrt(ms + EPS) * hg
        if post_scale is not None:
            y = y * post_scale
        o_ref[:, sl] = y.astype(o_ref.dtype)


def _mr_kernel(n_p, coef, *refs):
    lhs = refs[:n_p]
    ws = refs[n_p:2 * n_p]
    x_ref, gate_ref, o_ref = refs[2 * n_p:]
    acc = jnp.dot(lhs[0][...], ws[0][...].astype(BF16), preferred_element_type=F32)
    for p in range(1, n_p):
        acc = acc + jnp.dot(lhs[p][...], ws[p][...].astype(BF16), preferred_element_type=F32)
    gate = gate_ref[0]
    if coef != 1.0:
        gate = coef * gate
    o_ref[...] = x_ref[...] + gate * acc


def matmul_residual(lhs_list, w_list, lhs_offs, x, gate, coef, *, rows_per_mod, tm, tn, name="matmul_residual"):
    t, d = x.shape
    assert t % tm == 0 and d % tn == 0 and rows_per_mod % tm == 0
    in_specs = []
    for l, (_, _, _, kp), off in zip(lhs_list, w_list, lhs_offs):
        assert off % kp == 0
        in_specs.append(pl.BlockSpec((tm, kp), functools.partial(lambda i, j, o: (i, o), o=off // kp)))
    for _, prefix, rblk, kp in w_list:
        in_specs.append(pl.BlockSpec((None,) * len(prefix) + (kp, tn),
                                     functools.partial(lambda i, j, p, r: p + (r, j), p=prefix, r=rblk)))
    w_list = [w for w, _, _, _ in w_list]
    in_specs.append(pl.BlockSpec((tm, tn), lambda i, j: (i, j)))
    in_specs.append(pl.BlockSpec((1, 1, tn), lambda i, j: ((i * tm) // rows_per_mod, 0, j)))
    return pl.pallas_call(
        functools.partial(_mr_kernel, len(lhs_list), coef),
        grid=(t // tm, d // tn),
        in_specs=in_specs,
        out_specs=pl.BlockSpec((tm, tn), lambda i, j: (i, j)),
        out_shape=jax.ShapeDtypeStruct((t, d), F32),
        compiler_params=_cparams(("parallel", "parallel")),
        name=name,
    )(*lhs_list, *w_list, x, gate)


def _shift_rows(v, s):
    n = v.shape[0]
    r = pltpu.roll(v, s % n, axis=0)
    row = lax.broadcasted_iota(jnp.int32, v.shape, 0)
    if s > 0:
        return jnp.where(row < s, 0.0, r)
    return jnp.where(row >= n + s, 0.0, r)


def _conv3(v, w_ref):
    return (w_ref[0:1, :] * _shift_rows(v, 1) + w_ref[1:2, :] * v) + w_ref[2:3, :] * _shift_rows(v, -1)


def _conv3_silu_kernel(n_c, xc_ref, xx_ref, w_ref, b_ref, o_ref):
    for src, lo, ln in ((xc_ref, 0, n_c), (xx_ref, n_c, xx_ref.shape[1])):
        y = _conv3(src[0], w_ref) + b_ref[...]
        o_ref[0, lo:lo + ln, :] = _silu(y)


def conv3_silu_joint(xc, xx, w, b, ct=256):
    bsz, lc, ch = xc.shape
    lx = xx.shape[1]
    return pl.pallas_call(
        functools.partial(_conv3_silu_kernel, lc),
        grid=(bsz, ch // ct),
        in_specs=[pl.BlockSpec((1, lc, ct), lambda b_, j: (b_, 0, j)),
                  pl.BlockSpec((1, lx, ct), lambda b_, j: (b_, 0, j)),
                  pl.BlockSpec((3, ct), lambda b_, j: (0, j)),
                  pl.BlockSpec((1, ct), lambda b_, j: (0, j))],
        out_specs=pl.BlockSpec((1, lc + lx, ct), lambda b_, j: (b_, 0, j)),
        out_shape=jax.ShapeDtypeStruct((bsz, lc + lx, ch), F32),
        compiler_params=_cparams(("parallel", "parallel")),
        name="ssd_conv3_silu",
    )(xc, xx, w, b)


def _gated_conv3_kernel(p_ref, gate_ref, w_ref, o_ref):
    o_ref[0] = (gate_ref[0] * _conv3(p_ref[0], w_ref)).astype(o_ref.dtype)


def gated_conv3(prod, gate, w, ct=256):
    bsz, ln, ch = prod.shape
    spec = pl.BlockSpec((1, ln, ct), lambda b_, j: (b_, 0, j))
    return pl.pallas_call(
        _gated_conv3_kernel,
        grid=(bsz, ch // ct),
        in_specs=[spec, spec, pl.BlockSpec((3, ct), lambda b_, j: (0, j))],
        out_specs=spec,
        out_shape=jax.ShapeDtypeStruct((bsz, ln, ch), BF16),
        compiler_params=_cparams(("parallel", "parallel")),
        name="short_gated_conv",
    )(prod, gate, w)


CONV_HALO = 16
CONV_ROWS = 128
SUBLANES = 8
LANES = 128


def _conv31_kernel(n_c, uc_ref, ux_ref, w_ref, b_ref, o_ref, sh_ref):
    ct = o_ref.shape[2]

    def run(src_ref, lo, ln):
        lp = ln + 2 * CONV_HALO
        zeros = jnp.zeros((CONV_HALO, ct), F32)
        sh_ref[0, 0:CONV_HALO, :] = zeros
        sh_ref[0, CONV_HALO:CONV_HALO + ln, :] = src_ref[0]
        sh_ref[0, CONV_HALO + ln:lp, :] = zeros
        p = sh_ref[0, 0:lp, :]
        for s in range(1, SUBLANES):
            sh_ref[s, 0:lp, :] = pltpu.roll(p, lp - s, axis=0)

        def body(ci, carry):
            r0 = pl.multiple_of(ci * CONV_ROWS, CONV_ROWS)
            acc = jnp.zeros((CONV_ROWS, ct), F32) + b_ref[...]
            for k in range(CM_CONV):
                a, s = divmod(k + CONV_HALO - (CM_CONV - 1) // 2, SUBLANES)
                start = pl.multiple_of(r0 + SUBLANES * a, SUBLANES)
                acc = acc + sh_ref[s, pl.ds(start, CONV_ROWS), :] * w_ref[k:k + 1, :]
            o_ref[0, pl.ds(pl.multiple_of(lo + r0, CONV_ROWS), CONV_ROWS), :] = acc
            return carry
        lax.fori_loop(0, ln // CONV_ROWS, body, 0)

    run(uc_ref, 0, n_c)
    run(ux_ref, n_c, ux_ref.shape[1])


def conv31_joint(uc, ux, w, b, ct=128):
    bsz, lc, ch = uc.shape
    lx = ux.shape[1]
    return pl.pallas_call(
        functools.partial(_conv31_kernel, lc),
        grid=(bsz, ch // ct),
        in_specs=[pl.BlockSpec((1, lc, ct), lambda b_, j: (b_, 0, j)),
                  pl.BlockSpec((1, lx, ct), lambda b_, j: (b_, 0, j)),
                  pl.BlockSpec((CM_CONV, ct), lambda b_, j: (0, j)),
                  pl.BlockSpec((1, ct), lambda b_, j: (0, j))],
        out_specs=pl.BlockSpec((1, lc + lx, ct), lambda b_, j: (b_, 0, j)),
        out_shape=jax.ShapeDtypeStruct((bsz, lc + lx, ch), F32),
        scratch_shapes=[pltpu.VMEM((SUBLANES, lx + 2 * CONV_HALO, ct), F32)],
        compiler_params=_cparams(("parallel", "parallel")),
        name="conformer_conv31",
    )(uc, ux, w, b)


def _ssd_kernel(xs_ref, b_ref, c_ref, dt_ref, dtt_ref, ar_ref, ac_ref, y_ref, state_ref):
    q = SSD_CHUNK
    sgn = 1 - 2 * pl.program_id(0)

    @pl.when(pl.program_id(2) == 0)
    def _():
        state_ref[...] = jnp.zeros(state_ref.shape, F32)

    dtc = dt_ref[0, 0]
    dtr = dtt_ref[0, 0]
    da_c = dtc * ar_ref[0]
    da_r = dtr * ac_ref[0]
    ri = lax.broadcasted_iota(jnp.int32, (q, q), 0)
    ci = lax.broadcasted_iota(jnp.int32, (q, q), 1)
    keep = (ci - ri) * sgn <= 0
    tri_c = jnp.where(keep, 1.0, 0.0)
    tri_r = jnp.where((ri - ci) * sgn <= 0, 1.0, 0.0)
    acum_c = jnp.dot(tri_c, da_c, preferred_element_type=F32, precision=lax.Precision.HIGHEST)
    acum_r = jnp.dot(da_r, tri_r, preferred_element_type=F32, precision=lax.Precision.HIGHEST)
    alast_c = jnp.sum(da_c, axis=0, keepdims=True)
    pw = 2 * SSD_HEAD_DIM
    first_q = lax.broadcasted_iota(jnp.int32, (q, pw), 1) < SSD_HEAD_DIM
    first_1 = lax.broadcasted_iota(jnp.int32, (1, pw), 1) < SSD_HEAD_DIM

    for g in range(SSD_GROUPS):
        bg = b_ref[0, :, g * SSD_STATE:(g + 1) * SSD_STATE].astype(BF16)
        cg = c_ref[0, :, g * SSD_STATE:(g + 1) * SSD_STATE].astype(BF16)
        cb = lax.dot_general(cg, bg, (((1,), (1,)), ((), ())), preferred_element_type=F32)
        h_prev = state_ref[g]
        y_off = jnp.dot(cg, h_prev.astype(BF16), preferred_element_type=F32)
        xw, dec = [], []
        for pr in range(SSD_HPG // 2):
            pcol = slice(g * SSD_GW + pr * pw, g * SSD_GW + (pr + 1) * pw)
            xs_p = xs_ref[0, :, pcol]
            xs_b = xs_p.astype(BF16)
            y_d, e_ac, w_st, e_al = [], [], [], []
            for r in (2 * pr, 2 * pr + 1):
                h = g * SSD_HPG + r
                ac = acum_c[:, h:h + 1]
                ar = acum_r[h:h + 1, :]
                decay = jnp.exp(jnp.where(keep, ac - ar, NEG_BIG))
                mix = (cb * decay * dtr[h:h + 1, :]).astype(BF16)
                y_d.append(jnp.dot(mix, xs_b, preferred_element_type=F32))
                al = alast_c[:, h:h + 1]
                e_ac.append(jnp.exp(ac))
                w_st.append(jnp.exp(al - ac) * dtc[:, h:h + 1])
                e_al.append(jnp.exp(al))
            y_ref[0, 0, :, pcol] = (jnp.where(first_q, y_d[0], y_d[1])
                                    + y_off[:, pr * pw:(pr + 1) * pw] * jnp.where(first_q, e_ac[0], e_ac[1]))
            xw.append((xs_p * jnp.where(first_q, w_st[0], w_st[1])).astype(BF16))
            dec.append(jnp.where(first_1, e_al[0], e_al[1]))
        s_new = lax.dot_general(bg, jnp.concatenate(xw, axis=1), (((0,), (0,)), ((), ())),
                                preferred_element_type=F32)
        state_ref[g] = h_prev * jnp.concatenate(dec, axis=1) + s_new


def ssd_scan_joint(xbc, dt, dtt, a_row, a_col, n_ctx_chunks):
    bsz, ln, _ = xbc.shape
    nc = ln // SSD_CHUNK
    q = SSD_CHUNK

    def pos(d, c):
        back = jnp.where(c < n_ctx_chunks, n_ctx_chunks - 1 - c, nc + n_ctx_chunks - 1 - c)
        return jnp.where(d == 0, c, back)

    return pl.pallas_call(
        _ssd_kernel,
        grid=(2, bsz, nc),
        in_specs=[pl.BlockSpec((1, q, SSD_WIDTH), lambda d, b, c: (b, pos(d, c), 0)),
                  pl.BlockSpec((1, q, SSD_GN), lambda d, b, c: (b, pos(d, c), SSD_WIDTH // SSD_GN)),
                  pl.BlockSpec((1, q, SSD_GN), lambda d, b, c: (b, pos(d, c), SSD_WIDTH // SSD_GN + 1)),
                  pl.BlockSpec((1, 1, q, SSD_HEADS), lambda d, b, c: (d, b, pos(d, c), 0)),
                  pl.BlockSpec((1, 1, SSD_HEADS, q), lambda d, b, c: (d, b, 0, pos(d, c))),
                  pl.BlockSpec((1, 1, SSD_HEADS), lambda d, b, c: (d, 0, 0)),
                  pl.BlockSpec((1, SSD_HEADS, 1), lambda d, b, c: (d, 0, 0))],
        out_specs=pl.BlockSpec((1, 1, q, SSD_WIDTH), lambda d, b, c: (d, b, pos(d, c), 0)),
        out_shape=jax.ShapeDtypeStruct((2, bsz, ln, SSD_WIDTH), F32),
        scratch_shapes=[pltpu.VMEM((SSD_GROUPS, SSD_STATE, SSD_GW), F32)],
        compiler_params=_cparams(("parallel", "parallel", "arbitrary")),
        name="ssd_scan",
    )(xbc, xbc, xbc, dt, dtt, a_row, a_col)


EO_ROWS = 16


def _even_out_kernel(tq, yf_ref, yb_ref, xs_ref, zs_ref, cu_ref, dv_ref, ng_ref, lg_ref, lb_ref, o_ref):
    gw = SSD_WIDTH // SSD_GROUPS

    def body(r, carry):
        rows = pl.ds(pl.multiple_of(r * EO_ROWS, EO_ROWS), EO_ROWS)
        y = yf_ref[0, 0, rows, :] + yb_ref[0, 0, rows, :]
        y = y + dv_ref[...] * xs_ref[0, rows, :]
        y = y * zs_ref[rows, :]
        for g in range(SSD_GROUPS):
            sl = slice(g * gw, (g + 1) * gw)
            yg = y[:, sl]
            ms = jnp.mean(yg * yg, axis=-1, keepdims=True)
            o_ref[rows, sl] = (yg * lax.rsqrt(ms + EPS) * ng_ref[:, sl]).astype(o_ref.dtype)
        cv = cu_ref[0, rows, :]
        mu = jnp.mean(cv, axis=-1, keepdims=True)
        xc = cv - mu
        var = jnp.mean(xc * xc, axis=-1, keepdims=True)
        ln = xc * lax.rsqrt(var + EPS) * lg_ref[...] + lb_ref[...]
        o_ref[rows, SSD_WIDTH:SSD_WIDTH + CM_WIDTH] = _silu(ln).astype(o_ref.dtype)
        return carry
    lax.fori_loop(0, tq // EO_ROWS, body, 0, unroll=2)


def even_out(y, xbc, zs, cu, dvec, ng, lg, lb, *, blk0, nblk, tq):
    bsz = xbc.shape[0]
    vec = pl.BlockSpec((1, SSD_WIDTH), lambda b, i: (0, 0))
    return pl.pallas_call(
        functools.partial(_even_out_kernel, tq),
        grid=(bsz, nblk),
        in_specs=[pl.BlockSpec((1, 1, tq, SSD_WIDTH), lambda b, i: (0, b, blk0 + i, 0)),
                  pl.BlockSpec((1, 1, tq, SSD_WIDTH), lambda b, i: (1, b, blk0 + i, 0)),
                  pl.BlockSpec((1, tq, SSD_WIDTH), lambda b, i: (b, blk0 + i, 0)),
                  pl.BlockSpec((tq, SSD_WIDTH), lambda b, i: (b * nblk + i, 0)),
                  pl.BlockSpec((1, tq, CM_WIDTH), lambda b, i: (b, blk0 + i, 0)),
                  vec, vec, vec, vec],
        out_specs=pl.BlockSpec((tq, SSD_WIDTH + CM_WIDTH), lambda b, i: (b * nblk + i, 0)),
        out_shape=jax.ShapeDtypeStruct((bsz * nblk * tq, SSD_WIDTH + CM_WIDTH), BF16),
        compiler_params=_cparams(("parallel", "parallel")),
        name="even_mixer_tail",
    )(y, y, xbc, zs, cu, dvec, ng, lg, lb)


_NT = (((1,), (1,)), ((), ()))


def _na_kernel(rows_total, q_ref, k_ref, v_ref, kc_ref, vc_ref, bias_ref, o_ref):
    rb = pl.program_id(2)
    k0 = jnp.clip(rb * NA_QROWS - NA_ROWS // 2, 0, rows_total - NA_KROWS) * GRID_W
    k0 = pl.multiple_of(k0, GRID_W)
    nk = NA_KROWS * GRID_W
    for h in range(NA_HPS):
        hc = slice(h * NA_HEAD_DIM, (h + 1) * NA_HEAD_DIM)
        qv = q_ref[:, hc]
        s = lax.dot_general(qv, k_ref[pl.ds(k0, nk), hc], _NT, preferred_element_type=F32) + bias_ref[0, h]
        sc = lax.dot_general(qv, kc_ref[:, hc], _NT, preferred_element_type=F32)
        m = jnp.maximum(jnp.max(s, axis=-1, keepdims=True), jnp.max(sc, axis=-1, keepdims=True))
        p = jnp.exp(s - m)
        pc = jnp.exp(sc - m)
        den = jnp.sum(p, axis=-1, keepdims=True) + jnp.sum(pc, axis=-1, keepdims=True)
        o = jnp.dot(p.astype(BF16), v_ref[pl.ds(k0, nk), hc], preferred_element_type=F32)
        o = o + jnp.dot(pc.astype(BF16), vc_ref[:, hc], preferred_element_type=F32)
        o_ref[:, hc] = (o / den).astype(o_ref.dtype)


def neighbourhood_attention(q, k, v, kc, vc, bias, bsz, seq, n_ctx):
    rows = seq // GRID_W
    nrb = rows // NA_QROWS
    tq = NA_QROWS * GRID_W
    nk = NA_KROWS * GRID_W
    hw = NA_HPS * NA_HEAD_DIM

    def btype(rb):
        return jnp.where(rb == 0, 0, jnp.where(rb == nrb - 1, 2, 1))

    return pl.pallas_call(
        functools.partial(_na_kernel, rows),
        grid=(NA_HEADS // NA_HPS, bsz, nrb),
        in_specs=[pl.BlockSpec((tq, hw), lambda h, b, r: (b * nrb + r, h)),
                  pl.BlockSpec((seq, hw), lambda h, b, r: (b, h)),
                  pl.BlockSpec((seq, hw), lambda h, b, r: (b, h)),
                  pl.BlockSpec((n_ctx, hw), lambda h, b, r: (b, h)),
                  pl.BlockSpec((n_ctx, hw), lambda h, b, r: (b, h)),
                  pl.BlockSpec((1, NA_HPS, tq, nk), lambda h, b, r: (btype(r), h, 0, 0))],
        out_specs=pl.BlockSpec((tq, hw), lambda h, b, r: (b * nrb + r, h)),
        out_shape=jax.ShapeDtypeStruct((bsz * seq, NA_WIDTH), BF16),
        compiler_params=_cparams(("parallel", "parallel", "arbitrary")),
        name="neighbourhood_attention",
    )(q, k, v, kc, vc, bias)


def _ctx_attn_kernel(q_ref, k_ref, v_ref, o_ref):
    s = lax.dot_general(q_ref[...], k_ref[...], _NT, preferred_element_type=F32)
    m = jnp.max(s, axis=-1, keepdims=True)
    p = jnp.exp(s - m)
    den = jnp.sum(p, axis=-1, keepdims=True)
    o = jnp.dot(p.astype(BF16), v_ref[...], preferred_element_type=F32)
    o_ref[...] = (o / den).astype(o_ref.dtype)


def context_attention(q, k, v, bsz, n_ctx):
    spec = pl.BlockSpec((n_ctx, NA_HEAD_DIM), lambda b, h: (b, h))
    return pl.pallas_call(
        _ctx_attn_kernel,
        grid=(bsz, NA_HEADS),
        in_specs=[spec, spec, spec],
        out_specs=spec,
        out_shape=jax.ShapeDtypeStruct((bsz * n_ctx, NA_WIDTH), BF16),
        compiler_params=_cparams(("parallel", "parallel")),
        name="context_attention",
    )(q, k, v)


def _na_bias_tables(rpb, rows):
    nrb = rows // NA_QROWS
    col = np.arange(GRID_W)
    c0 = np.clip(col - NA_COLS // 2, 0, GRID_W - NA_COLS)
    col_ok = (col[None, :] >= c0[:, None]) & (col[None, :] < c0[:, None] + NA_COLS)
    col_idx = np.clip(col[None, :] - col[:, None] + NA_COLS - 1, 0, 2 * NA_COLS - 2)
    kr = min(NA_ROWS, rows)
    ridx = np.zeros((3, NA_QROWS, NA_KROWS), np.int32)
    rok = np.zeros((3, NA_QROWS, NA_KROWS), bool)
    for t, rb in enumerate((0, 1, nrb - 1)):
        k0 = int(np.clip(rb * NA_QROWS - NA_ROWS // 2, 0, rows - NA_KROWS))
        for rq in range(NA_QROWS):
            r = rb * NA_QROWS + rq
            r0 = int(np.clip(r - kr // 2, 0, rows - kr))
            for rk in range(NA_KROWS):
                ra = k0 + rk
                rok[t, rq, rk] = r0 <= ra < r0 + kr
                ridx[t, rq, rk] = int(np.clip(ra - r + NA_ROWS - 1, 0, 2 * NA_ROWS - 2))
    n_off = 2 * NA_ROWS - 1
    tz = rpb[:, :, col_idx].astype(F32)
    tz = jnp.where(col_ok[None, None], tz, NEG_BIG)
    tz = jnp.concatenate([tz, jnp.full((NA_HEADS, 1, GRID_W, GRID_W), NEG_BIG, F32)], axis=1)
    tile_of = np.where(rok, ridx, n_off)

    def build(tz_ref, o_ref):
        for t in range(3):
            @pl.when(pl.program_id(0) == t)
            def _(t=t):
                for rq in range(NA_QROWS):
                    for pr in range(NA_KROWS // 2):
                        a, b = int(tile_of[t, rq, 2 * pr]), int(tile_of[t, rq, 2 * pr + 1])
                        o_ref[0, 0, rq * GRID_W:(rq + 1) * GRID_W, 2 * pr * GRID_W:(2 * pr + 2) * GRID_W] = (
                            jnp.concatenate([tz_ref[0, a], tz_ref[0, b]], axis=1))

    return pl.pallas_call(
        build,
        grid=(3, NA_HEADS),
        in_specs=[pl.BlockSpec((1, n_off + 1, GRID_W, GRID_W), lambda t, h: (h, 0, 0, 0))],
        out_specs=pl.BlockSpec((1, 1, NA_QROWS * GRID_W, NA_KROWS * GRID_W), lambda t, h: (t, h, 0, 0)),
        out_shape=jax.ShapeDtypeStruct((3, NA_HEADS, NA_QROWS * GRID_W, NA_KROWS * GRID_W), F32),
        compiler_params=_cparams(("parallel", "parallel")),
        name="na_bias_tables",
    )(tz)


class _Stream:
    def __init__(self, mod, rows_per_mod, tm):
        self.mod = mod
        self.rows_per_mod = rows_per_mod
        self.tm = tm

    def m(self, idx):
        return self.mod[:, idx][:, None, :]


def _ffn_half(h, st, g, w_in, w_out, j):
    xn = adaln(h, st.m(3 * j), st.m(3 * j + 1), g, rows_per_mod=st.rows_per_mod)
    hid = proj_matmul(xn, [w_in, w_in], [0, FFN_DIM], FFN_DIM, _ep_swiglu, BF16, tm=st.tm, tn=512, name="ffn_in")
    return matmul_residual([hid], [w_out + (0, FFN_DIM)], [0], h, st.m(3 * j + 2), 0.5,
                           rows_per_mod=st.rows_per_mod, tm=st.tm, tn=512, name="ffn_out")


def _proj(xn, st, w, names, n_out, ep, dtype, extras=(), name="mixer_in"):
    tn = min(n_out, 1024 if len(names) == 1 else 512)
    return proj_matmul(xn, [w[n][0] for n in names], [w[n][1] for n in names], n_out, ep, dtype,
                       tm=st.tm, tn=tn, extras=extras, name=name)


def _mixer_even(x, cx, sx, sc, g, p, bsz, seq, n_ctx):
    w = p["w_in"]
    outs = {}
    for key, h, st in (("x", x, sx), ("c", cx, sc)):
        xn = adaln(h, st.m(3), st.m(4), g, rows_per_mod=st.rows_per_mod)
        zs = _proj(xn, st, w, ("z",), SSD_WIDTH, _ep_silu, F32, name="ssd_z")
        xbc = _proj(xn, st, w, ("xbc",), SSD_XBC, _ep_id, F32, name="ssd_xbc")
        dt = _proj(xn, st, w, ("dt",), 2 * SSD_HEADS, _ep_softplus_bias, F32, extras=(p["dt_bias"],), name="ssd_dt")
        u = _proj(xn, st, w, ("ga", "gg"), CM_WIDTH, _ep_glu_sigmoid, F32, name="conformer_glu")
        outs[key] = (zs, xbc, dt, u)
    zs_x, xbc_x, dt_x, u_x = outs["x"]
    zs_c, xbc_c, dt_c, u_c = outs["c"]
    xbc = conv3_silu_joint(xbc_c.reshape(bsz, n_ctx, SSD_XBC), xbc_x.reshape(bsz, seq, SSD_XBC),
                           p["conv_w"], p["conv_b"])
    dt = jnp.concatenate([dt_c.reshape(bsz, n_ctx, 2, SSD_HEADS), dt_x.reshape(bsz, seq, 2, SSD_HEADS)], axis=1)
    dt = jnp.transpose(dt, (2, 0, 1, 3))
    dtt = jnp.transpose(dt, (0, 1, 3, 2))
    y = ssd_scan_joint(xbc, dt, dtt, p["a_row"], p["a_col"], n_ctx // SSD_CHUNK)
    cu = conv31_joint(u_c.reshape(bsz, n_ctx, CM_WIDTH), u_x.reshape(bsz, seq, CM_WIDTH), p["cm_w"], p["cm_b"])
    tq = n_ctx
    tail = functools.partial(even_out, y, xbc, dvec=p["dvec"], ng=p["norm_g"], lg=p["ln_g"], lb=p["ln_b"], tq=tq)
    lhs_x = tail(zs=zs_x, cu=cu, blk0=1, nblk=seq // tq)
    lhs_c = tail(zs=zs_c, cu=cu, blk0=0, nblk=1)
    return lhs_x, lhs_c


def _mixer_odd(x, cx, sx, sc, g, p, bsz, seq, n_ctx, need_ctx):
    w = p["w_in"]
    hg_q, hg_k = p["q_g"], p["k_g"]
    ep_q = functools.partial(_ep_head_rms, NA_HEAD_DIM ** -0.5)
    ep_k = functools.partial(_ep_head_rms, None)
    xn_x = adaln(x, sx.m(3), sx.m(4), g, rows_per_mod=sx.rows_per_mod)
    xn_c = adaln(cx, sc.m(3), sc.m(4), g, rows_per_mod=sc.rows_per_mod)
    kc = _proj(xn_c, sc, w, ("k",), NA_WIDTH, ep_k, BF16, extras=(hg_k,), name="na_k")
    vc = _proj(xn_c, sc, w, ("v",), NA_WIDTH, _ep_id, BF16, name="na_v")
    q = _proj(xn_x, sx, w, ("q",), NA_WIDTH, ep_q, BF16, extras=(hg_q,), name="na_q")
    k = _proj(xn_x, sx, w, ("k",), NA_WIDTH, ep_k, BF16, extras=(hg_k,), name="na_k")
    v = _proj(xn_x, sx, w, ("v",), NA_WIDTH, _ep_id, BF16, name="na_v")
    gb = _proj(xn_x, sx, w, ("gb",), SC_WIDTH, _ep_id, F32, name="sc_gate")
    pr = _proj(xn_x, sx, w, ("gc", "hs"), SC_WIDTH, _ep_mul, F32, name="sc_prod")
    ox = neighbourhood_attention(q, k, v, kc, vc, p["bias"], bsz, seq, n_ctx)
    yx = gated_conv3(pr.reshape(bsz, seq, SC_WIDTH), gb.reshape(bsz, seq, SC_WIDTH), p["sc_w"])
    yx = yx.reshape(bsz * seq, SC_WIDTH)
    if not need_ctx:
        return (ox, yx), None
    qc = _proj(xn_c, sc, w, ("q",), NA_WIDTH, ep_q, BF16, extras=(hg_q,), name="na_q")
    gbc = _proj(xn_c, sc, w, ("gb",), SC_WIDTH, _ep_id, F32, name="sc_gate")
    prc = _proj(xn_c, sc, w, ("gc", "hs"), SC_WIDTH, _ep_mul, F32, name="sc_prod")
    oc = context_attention(qc, kc, vc, bsz, n_ctx)
    yc = gated_conv3(prc.reshape(bsz, n_ctx, SC_WIDTH), gbc.reshape(bsz, n_ctx, SC_WIDTH), p["sc_w"])
    yc = yc.reshape(bsz * n_ctx, SC_WIDTH)
    return (ox, yx), (oc, yc)


def _segments(w, prefix, names, sizes, tn=512):
    out, acc = {}, 0
    for n, s in zip(names, sizes):
        if acc % tn == 0 and s % LANES == 0:
            out[n] = ((w, prefix), acc)
        else:
            out[n] = ((w[prefix][:, acc:acc + s], ()), 0)
        acc += s
    return out


def kernel(x, c, ctx, c_ctx, w_mod, b_mod, norm_g, ffn_w_in, ffn_w_out, ev_w_in, ev_w_out, ssd_conv_w, ssd_conv_b,
           ssd_dt_bias, ssd_a_log, ssd_d, ssd_norm_g, cm_conv_w, cm_conv_b, cm_ln_g, cm_ln_b, od_w_in, od_w_out,
           na_q_g, na_k_g, na_rpb, sc_conv_w):
    bsz, seq, d = x.shape
    n_ctx = ctx.shape[1]
    rows = seq // GRID_W

    cvec = jnp.concatenate([c, c_ctx[None, :], jnp.zeros((8 - bsz - 1, d), F32)], axis=0)
    mod = modulation(cvec, w_mod, b_mod).reshape(DEPTH, 8, N_MOD, d)

    ffn_out_b = ffn_w_out.astype(BF16)
    ev_out_b = ev_w_out.astype(BF16)
    od_out_b = od_w_out.astype(BF16)

    xs = x.reshape(bsz * seq, d)
    cs = ctx.reshape(bsz * n_ctx, d)
    for i in range(DEPTH):
        last = i == DEPTH - 1
        sx = _Stream(mod[i, :bsz], seq, 1024)
        sc = _Stream(mod[i, bsz:bsz + 1], bsz * n_ctx, bsz * n_ctx)
        g = norm_g[i]
        xs = _ffn_half(xs, sx, g[0:1], (ffn_w_in, (i, 0)), (ffn_out_b, (i,0)), 0)
        cs = _ffn_half(cs, sc, g[0:1], (ffn_w_in, (i, 0)), (ffn_out_b, (i,0)), 0)
        j = i // 2
        if i % 2 == 0:
            names = ("z", "xbc", "dt", "ga", "gg")
            sizes = (SSD_WIDTH, SSD_XBC, 2 * SSD_HEADS, CM_WIDTH, CM_WIDTH)
            a_neg = -jnp.exp(ssd_a_log[j].astype(F32))
            p = {
                "w_in": _segments(ev_w_in, (j,), names, sizes),
                "dt_bias": ssd_dt_bias[j].reshape(1, 2 * SSD_HEADS).astype(F32),
                "conv_w": ssd_conv_w[j], "conv_b": ssd_conv_b[j][None, :],
                "a_row": a_neg[:, None, :], "a_col": a_neg[:, :, None],
                "cm_w": cm_conv_w[j], "cm_b": cm_conv_b[j][None, :],
                "dvec": jnp.repeat(ssd_d[j], SSD_HEAD_DIM)[None, :],
                "norm_g": ssd_norm_g[j][None, :], "ln_g": cm_ln_g[j][None, :], "ln_b": cm_ln_b[j][None, :],
            }
            lhs_x, lhs_c = _mixer_even(xs, cs, sx, sc, g[1:2], p, bsz, seq, n_ctx)
            w_o = [(ev_out_b, (j,), 0, SSD_WIDTH + CM_WIDTH)]
            xs = matmul_residual([lhs_x], w_o, [0], xs, sx.m(5), 1.0,
                                 rows_per_mod=sx.rows_per_mod, tm=sx.tm, tn=512, name="mixer_out")
            cs_mix = [lhs_c], w_o, [0]
        else:
            names = ("q", "k", "v", "gb", "gc", "hs")
            p = {
                "w_in": _segments(od_w_in, (j,), names, (NA_WIDTH,) * 3 + (SC_WIDTH,) * 3),
                "q_g": na_q_g[j][None, :], "k_g": na_k_g[j][None, :],
                "bias": _na_bias_tables(na_rpb[j], rows), "sc_w": sc_conv_w[j],
            }
            lx, lc = _mixer_odd(xs, cs, sx, sc, g[1:2], p, bsz, seq, n_ctx, not last)
            w_o = [(od_out_b, (j,), 0, NA_WIDTH), (od_out_b, (j,), 1, SC_WIDTH)]
            xs = matmul_residual(list(lx), w_o, [0, 0], xs, sx.m(5), 1.0,
                                 rows_per_mod=sx.rows_per_mod, tm=sx.tm, tn=512, name="mixer_out")
            cs_mix = (list(lc), w_o, [0, 0]) if lc is not None else None
        xs = _ffn_half(xs, sx, g[2:3], (ffn_w_in, (i, 1)), (ffn_out_b, (i,1)), 2)
        if not last:
            cs = matmul_residual(cs_mix[0], cs_mix[1], cs_mix[2], cs, sc.m(5), 1.0,
                                 rows_per_mod=sc.rows_per_mod, tm=sc.tm, tn=512, name="mixer_out")
            cs = _ffn_half(cs, sc, g[2:3], (ffn_w_in, (i, 1)), (ffn_out_b, (i,1)), 2)
    return xs.reshape(bsz, seq, d)
```

```python
import functools

import numpy as np
import jax
import jax.numpy as jnp
from jax import lax
from jax.experimental import pallas as pl
from jax.experimental.pallas import tpu as pltpu

F32 = jnp.float32
BF16 = jnp.bfloat16

D_MODEL = 2048
DEPTH = 4
GRID_W = 64
N_MOD = 9
FFN_DIM = 5632
SSD_HEADS = 32
SSD_HEAD_DIM = 64
SSD_WIDTH = SSD_HEADS * SSD_HEAD_DIM
SSD_GROUPS = 8
SSD_STATE = 128
SSD_CHUNK = 128
SSD_GN = SSD_GROUPS * SSD_STATE
SSD_XBC = SSD_WIDTH + 2 * SSD_GN
SSD_HPG = SSD_HEADS // SSD_GROUPS
SSD_GW = SSD_HPG * SSD_HEAD_DIM
CM_WIDTH = 2048
CM_CONV = 31
NA_HEADS = 16
NA_HEAD_DIM = 128
NA_WIDTH = NA_HEADS * NA_HEAD_DIM
NA_ROWS = 8
NA_COLS = 16
SC_WIDTH = 2048
EPS = 1e-6

NA_QROWS = 8
NA_KROWS = 16
NA_HPS = 4
NEG_BIG = -1e30

VMEM_LIMIT = 56 * 1024 * 1024


def _cparams(sem):
    return pltpu.CompilerParams(dimension_semantics=sem, vmem_limit_bytes=VMEM_LIMIT)


def _silu(v):
    return v * jax.nn.sigmoid(v)


def _mod_kernel(c_ref, w_ref, b_ref, o_ref):
    s = _silu(c_ref[...]).astype(BF16)
    o_ref[0] = jnp.dot(s, w_ref[0].astype(BF16), preferred_element_type=F32) + b_ref[0]


def modulation(cvec, w_mod, b_mod, tn=1024):
    depth, d, n = w_mod.shape
    rows = cvec.shape[0]
    return pl.pallas_call(
        _mod_kernel,
        grid=(depth, n // tn),
        in_specs=[pl.BlockSpec((rows, d), lambda l, j: (0, 0)),
                  pl.BlockSpec((1, d, tn), lambda l, j: (l, 0, j)),
                  pl.BlockSpec((1, 1, tn), lambda l, j: (l, 0, j))],
        out_specs=pl.BlockSpec((1, rows, tn), lambda l, j: (l, 0, j)),
        out_shape=jax.ShapeDtypeStruct((depth, rows, n), F32),
        compiler_params=_cparams(("parallel", "parallel")),
        name="modulation",
    )(cvec, w_mod, b_mod.reshape(depth, 1, n))


NORM_ROWS = 16
NORM_UNROLL = 4


def _adaln_kernel(tm, x_ref, sh_ref, sc_ref, g_ref, o_ref):
    def body(r, carry):
        rows = pl.ds(pl.multiple_of(r * NORM_ROWS, NORM_ROWS), NORM_ROWS)
        xv = x_ref[rows, :]
        ms = jnp.mean(xv * xv, axis=-1, keepdims=True)
        y = xv * lax.rsqrt(ms + EPS) * g_ref[...]
        y = y * (1.0 + sc_ref[0]) + sh_ref[0]
        o_ref[rows, :] = y.astype(o_ref.dtype)
        return carry
    lax.fori_loop(0, tm // NORM_ROWS, body, 0, unroll=NORM_UNROLL)


def adaln(x, shift, scale, g, *, rows_per_mod):
    t, d = x.shape
    tm = min(1024, rows_per_mod)
    assert t % tm == 0 and rows_per_mod % tm == 0
    mod_spec = pl.BlockSpec((1, 1, d), lambda i: ((i * tm) // rows_per_mod, 0, 0))
    return pl.pallas_call(
        functools.partial(_adaln_kernel, tm),
        grid=(t // tm,),
        in_specs=[pl.BlockSpec((tm, d), lambda i: (i, 0)), mod_spec, mod_spec, pl.BlockSpec((1, d), lambda i: (0, 0))],
        out_specs=pl.BlockSpec((tm, d), lambda i: (i, 0)),
        out_shape=jax.ShapeDtypeStruct((t, d), BF16),
        compiler_params=_cparams(("parallel",)),
        name="adaln",
    )(x, shift, scale, g)


def _pm_kernel(n_w, n_extra, epilogue, w_transposed, xn_ref, *rest):
    w_refs = rest[:n_w]
    extra = rest[n_w:n_w + n_extra]
    o_ref = rest[n_w + n_extra]
    wb_refs = rest[n_w + n_extra + 1:]

    @pl.when(pl.program_id(1) == 0)
    def _():
        for w, wb in zip(w_refs, wb_refs):
            wv = w[...]
            wb[...] = (wv.T if w_transposed else wv).astype(BF16)

    xn = xn_ref[...]
    accs = [jnp.dot(xn, wb[...], preferred_element_type=F32) for wb in wb_refs]
    epilogue(accs, extra, o_ref)


def proj_matmul(xn, ws, col_offs, n_out, epilogue, out_dtype, *, tm, tn, extras=(), w_transposed=False,
                name="proj_matmul"):
    t, d = xn.shape
    assert t % tm == 0 and n_out % tn == 0
    for off in col_offs:
        assert off % tn == 0
    in_specs = [pl.BlockSpec((tm, d), lambda j, i: (i, 0))]
    for (_, prefix), off in zip(ws, col_offs):
        if w_transposed:
            in_specs.append(pl.BlockSpec((None,) * len(prefix) + (tn, d),
                                         functools.partial(lambda j, i, p, o: p + (j + o, 0), p=prefix, o=off // tn)))
        else:
            in_specs.append(pl.BlockSpec((None,) * len(prefix) + (d, tn),
                                         functools.partial(lambda j, i, p, o: p + (0, j + o), p=prefix, o=off // tn)))
    for e in extras:
        if e.shape[1] == n_out:
            in_specs.append(pl.BlockSpec((1, tn), lambda j, i: (0, j)))
        else:
            in_specs.append(pl.BlockSpec(e.shape, lambda j, i: (0, 0)))
    return pl.pallas_call(
        functools.partial(_pm_kernel, len(ws), len(extras), epilogue, w_transposed),
        grid=(n_out // tn, t // tm),
        in_specs=in_specs,
        out_specs=pl.BlockSpec((tm, tn), lambda j, i: (i, j)),
        out_shape=jax.ShapeDtypeStruct((t, n_out), out_dtype),
        scratch_shapes=[pltpu.VMEM((d, tn), BF16) for _ in ws],
        compiler_params=_cparams(("arbitrary", "arbitrary")),
        name=name,
    )(xn, *[w for w, _ in ws], *extras)


def _ep_swiglu(accs, extra, o_ref):
    a, g = accs
    o_ref[...] = (_silu(a) * g).astype(o_ref.dtype)


def _ep_glu_sigmoid(accs, extra, o_ref):
    a, g = accs
    o_ref[...] = (a * jax.nn.sigmoid(g)).astype(o_ref.dtype)


def _ep_mul(accs, extra, o_ref):
    a, g = accs
    o_ref[...] = (a * g).astype(o_ref.dtype)


def _ep_silu(accs, extra, o_ref):
    o_ref[...] = _silu(accs[0]).astype(o_ref.dtype)


def _ep_id(accs, extra, o_ref):
    o_ref[...] = accs[0].astype(o_ref.dtype)


def _ep_softplus_bias(accs, extra, o_ref):
    v = accs[0] + extra[0][...]
    o_ref[...] = jnp.maximum(v, 0.0) + jnp.log1p(jnp.exp(-jnp.abs(v)))


def _ep_head_rms(post_scale, accs, extra, o_ref):
    a = accs[0]
    hg = extra[0][...]
    for h in range(a.shape[1] // NA_HEAD_DIM):
        sl = slice(h * NA_HEAD_DIM, (h + 1) * NA_HEAD_DIM)
        ah = a[:, sl]
        ms = jnp.mean(ah * ah, axis=-1, keepdims=True)
        y = ah * lax.rsqrt(ms + EPS) * hg
        if post_scale is not None:
            y = y * post_scale
        o_ref[:, sl] = y.astype(o_ref.dtype)


def _mr_kernel(n_p, coef, *refs):
    lhs = refs[:n_p]
    ws = refs[n_p:2 * n_p]
    x_ref, gate_ref, o_ref = refs[2 * n_p:]
    acc = jnp.dot(lhs[0][...], ws[0][...].astype(BF16), preferred_element_type=F32)
    for p in range(1, n_p):
        acc = acc + jnp.dot(lhs[p][...], ws[p][...].astype(BF16), preferred_element_type=F32)
    gate = gate_ref[0]
    if coef != 1.0:
        gate = coef * gate
    o_ref[...] = x_ref[...] + gate * acc


def matmul_residual(lhs_list, w_list, lhs_offs, x, gate, coef, *, rows_per_mod, tm, tn, name="matmul_residual"):
    t, d = x.shape
    assert t % tm == 0 and d % tn == 0 and rows_per_mod % tm == 0
    in_specs = []
    for l, (_, _, _, kp), off in zip(lhs_list, w_list, lhs_offs):
        assert off % kp == 0
        in_specs.append(pl.BlockSpec((tm, kp), functools.partial(lambda i, j, o: (i, o), o=off // kp)))
    for _, prefix, rblk, kp in w_list:
        in_specs.append(pl.BlockSpec((None,) * len(prefix) + (kp, tn),
                                     functools.partial(lambda i, j, p, r: p + (r, j), p=prefix, r=rblk)))
    w_list = [w for w, _, _, _ in w_list]
    in_specs.append(pl.BlockSpec((tm, tn), lambda i, j: (i, j)))
    in_specs.append(pl.BlockSpec((1, 1, tn), lambda i, j: ((i * tm) // rows_per_mod, 0, j)))
    return pl.pallas_call(
        functools.partial(_mr_kernel, len(lhs_list), coef),
        grid=(t // tm, d // tn),
        in_specs=in_specs,
        out_specs=pl.BlockSpec((tm, tn), lambda i, j: (i, j)),
        out_shape=jax.ShapeDtypeStruct((t, d), F32),
        compiler_params=_cparams(("parallel", "parallel")),
        name=name,
    )(*lhs_list, *w_list, x, gate)


def _shift_rows(v, s):
    n = v.shape[0]
    r = pltpu.roll(v, s % n, axis=0)
    row = lax.broadcasted_iota(jnp.int32, v.shape, 0)
    if s > 0:
        return jnp.where(row < s, 0.0, r)
    return jnp.where(row >= n + s, 0.0, r)


def _conv3(v, w_ref):
    return (w_ref[0:1, :] * _shift_rows(v, 1) + w_ref[1:2, :] * v) + w_ref[2:3, :] * _shift_rows(v, -1)


def _conv3_silu_kernel(n_c, xc_ref, xx_ref, w_ref, b_ref, o_ref):
    for src, lo, ln in ((xc_ref, 0, n_c), (xx_ref, n_c, xx_ref.shape[1])):
        y = _conv3(src[0], w_ref) + b_ref[...]
        o_ref[0, lo:lo + ln, :] = _silu(y)


def conv3_silu_joint(xc, xx, w, b, ct=256):
    bsz, lc, ch = xc.shape
    lx = xx.shape[1]
    return pl.pallas_call(
        functools.partial(_conv3_silu_kernel, lc),
        grid=(bsz, ch // ct),
        in_specs=[pl.BlockSpec((1, lc, ct), lambda b_, j: (b_, 0, j)),
                  pl.BlockSpec((1, lx, ct), lambda b_, j: (b_, 0, j)),
                  pl.BlockSpec((3, ct), lambda b_, j: (0, j)),
                  pl.BlockSpec((1, ct), lambda b_, j: (0, j))],
        out_specs=pl.BlockSpec((1, lc + lx, ct), lambda b_, j: (b_, 0, j)),
        out_shape=jax.ShapeDtypeStruct((bsz, lc + lx, ch), F32),
        compiler_params=_cparams(("parallel", "parallel")),
        name="ssd_conv3_silu",
    )(xc, xx, w, b)


def _gated_conv3_kernel(p_ref, gate_ref, w_ref, o_ref):
    o_ref[0] = (gate_ref[0] * _conv3(p_ref[0], w_ref)).astype(o_ref.dtype)


def gated_conv3(prod, gate, w, ct=256):
    bsz, ln, ch = prod.shape
    spec = pl.BlockSpec((1, ln, ct), lambda b_, j: (b_, 0, j))
    return pl.pallas_call(
        _gated_conv3_kernel,
        grid=(bsz, ch // ct),
        in_specs=[spec, spec, pl.BlockSpec((3, ct), lambda b_, j: (0, j))],
        out_specs=spec,
        out_shape=jax.ShapeDtypeStruct((bsz, ln, ch), BF16),
        compiler_params=_cparams(("parallel", "parallel")),
        name="short_gated_conv",
    )(prod, gate, w)


CONV_HALO = 16
CONV_ROWS = 128
SUBLANES = 8
LANES = 128


def _conv31_kernel(n_c, uc_ref, ux_ref, w_ref, b_ref, o_ref, sh_ref):
    ct = o_ref.shape[2]

    def run(src_ref, lo, ln):
        lp = ln + 2 * CONV_HALO
        zeros = jnp.zeros((CONV_HALO, ct), F32)
        sh_ref[0, 0:CONV_HALO, :] = zeros
        sh_ref[0, CONV_HALO:CONV_HALO + ln, :] = src_ref[0]
        sh_ref[0, CONV_HALO + ln:lp, :] = zeros
        p = sh_ref[0, 0:lp, :]
        for s in range(1, SUBLANES):
            sh_ref[s, 0:lp, :] = pltpu.roll(p, lp - s, axis=0)

        def body(ci, carry):
            r0 = pl.multiple_of(ci * CONV_ROWS, CONV_ROWS)
            acc = jnp.zeros((CONV_ROWS, ct), F32) + b_ref[...]
            for k in range(CM_CONV):
                a, s = divmod(k + CONV_HALO - (CM_CONV - 1) // 2, SUBLANES)
                start = pl.multiple_of(r0 + SUBLANES * a, SUBLANES)
                acc = acc + sh_ref[s, pl.ds(start, CONV_ROWS), :] * w_ref[k:k + 1, :]
            o_ref[0, pl.ds(pl.multiple_of(lo + r0, CONV_ROWS), CONV_ROWS), :] = acc
            return carry
        lax.fori_loop(0, ln // CONV_ROWS, body, 0)

    run(uc_ref, 0, n_c)
    run(ux_ref, n_c, ux_ref.shape[1])


def conv31_joint(uc, ux, w, b, ct=128):
    bsz, lc, ch = uc.shape
    lx = ux.shape[1]
    return pl.pallas_call(
        functools.partial(_conv31_kernel, lc),
        grid=(bsz, ch // ct),
        in_specs=[pl.BlockSpec((1, lc, ct), lambda b_, j: (b_, 0, j)),
                  pl.BlockSpec((1, lx, ct), lambda b_, j: (b_, 0, j)),
                  pl.BlockSpec((CM_CONV, ct), lambda b_, j: (0, j)),
                  pl.BlockSpec((1, ct), lambda b_, j: (0, j))],
        out_specs=pl.BlockSpec((1, lc + lx, ct), lambda b_, j: (b_, 0, j)),
        out_shape=jax.ShapeDtypeStruct((bsz, lc + lx, ch), F32),
        scratch_shapes=[pltpu.VMEM((SUBLANES, lx + 2 * CONV_HALO, ct), F32)],
        compiler_params=_cparams(("parallel", "parallel")),
        name="conformer_conv31",
    )(uc, ux, w, b)


def _ssd_kernel(xs_ref, b_ref, c_ref, dt_ref, dtt_ref, ar_ref, ac_ref, y_ref, state_ref):
    q = SSD_CHUNK
    sgn = 1 - 2 * pl.program_id(0)

    @pl.when(pl.program_id(2) == 0)
    def _():
        state_ref[...] = jnp.zeros(state_ref.shape, F32)

    dtc = dt_ref[0, 0]
    dtr = dtt_ref[0, 0]
    da_c = dtc * ar_ref[0]
    da_r = dtr * ac_ref[0]
    ri = lax.broadcasted_iota(jnp.int32, (q, q), 0)
    ci = lax.broadcasted_iota(jnp.int32, (q, q), 1)
    keep = (ci - ri) * sgn <= 0
    tri_c = jnp.where(keep, 1.0, 0.0)
    tri_r = jnp.where((ri - ci) * sgn <= 0, 1.0, 0.0)
    acum_c = jnp.dot(tri_c, da_c, preferred_element_type=F32, precision=lax.Precision.HIGHEST)
    acum_r = jnp.dot(da_r, tri_r, preferred_element_type=F32, precision=lax.Precision.HIGHEST)
    alast_c = jnp.sum(da_c, axis=0, keepdims=True)
    pw = 2 * SSD_HEAD_DIM
    first_q = lax.broadcasted_iota(jnp.int32, (q, pw), 1) < SSD_HEAD_DIM
    first_1 = lax.broadcasted_iota(jnp.int32, (1, pw), 1) < SSD_HEAD_DIM

    for g in range(SSD_GROUPS):
        bg = b_ref[0, :, g * SSD_STATE:(g + 1) * SSD_STATE].astype(BF16)
        cg = c_ref[0, :, g * SSD_STATE:(g + 1) * SSD_STATE].astype(BF16)
        cb = lax.dot_general(cg, bg, (((1,), (1,)), ((), ())), preferred_element_type=F32)
        h_prev = state_ref[g]
        y_off = jnp.dot(cg, h_prev.astype(BF16), preferred_element_type=F32)
        xw, dec = [], []
        for pr in range(SSD_HPG // 2):
            pcol = slice(g * SSD_GW + pr * pw, g * SSD_GW + (pr + 1) * pw)
            xs_p = xs_ref[0, :, pcol]
            xs_b = xs_p.astype(BF16)
            y_d, e_ac, w_st, e_al = [], [], [], []
            for r in (2 * pr, 2 * pr + 1):
                h = g * SSD_HPG + r
                ac = acum_c[:, h:h + 1]
                ar = acum_r[h:h + 1, :]
                decay = jnp.exp(jnp.where(keep, ac - ar, NEG_BIG))
                mix = (cb * decay * dtr[h:h + 1, :]).astype(BF16)
                y_d.append(jnp.dot(mix, xs_b, preferred_element_type=F32))
                al = alast_c[:, h:h + 1]
                e_ac.append(jnp.exp(ac))
                w_st.append(jnp.exp(al - ac) * dtc[:, h:h + 1])
                e_al.append(jnp.exp(al))
            y_ref[0, 0, :, pcol] = (jnp.where(first_q, y_d[0], y_d[1])
                                    + y_off[:, pr * pw:(pr + 1) * pw] * jnp.where(first_q, e_ac[0], e_ac[1]))
            xw.append((xs_p * jnp.where(first_q, w_st[0], w_st[1])).astype(BF16))
            dec.append(jnp.where(first_1, e_al[0], e_al[1]))
        s_new = lax.dot_general(bg, jnp.concatenate(xw, axis=1), (((0,), (0,)), ((), ())),
                                preferred_element_type=F32)
        state_ref[g] = h_prev * jnp.concatenate(dec, axis=1) + s_new


def ssd_scan_joint(xbc, dt, dtt, a_row, a_col, n_ctx_chunks):
    bsz, ln, _ = xbc.shape
    nc = ln // SSD_CHUNK
    q = SSD_CHUNK

    def pos(d, c):
        back = jnp.where(c < n_ctx_chunks, n_ctx_chunks - 1 - c, nc + n_ctx_chunks - 1 - c)
        return jnp.where(d == 0, c, back)

    return pl.pallas_call(
        _ssd_kernel,
        grid=(2, bsz, nc),
        in_specs=[pl.BlockSpec((1, q, SSD_WIDTH), lambda d, b, c: (b, pos(d, c), 0)),
                  pl.BlockSpec((1, q, SSD_GN), lambda d, b, c: (b, pos(d, c), SSD_WIDTH // SSD_GN)),
                  pl.BlockSpec((1, q, SSD_GN), lambda d, b, c: (b, pos(d, c), SSD_WIDTH // SSD_GN + 1)),
                  pl.BlockSpec((1, 1, q, SSD_HEADS), lambda d, b, c: (d, b, pos(d, c), 0)),
                  pl.BlockSpec((1, 1, SSD_HEADS, q), lambda d, b, c: (d, b, 0, pos(d, c))),
                  pl.BlockSpec((1, 1, SSD_HEADS), lambda d, b, c: (d, 0, 0)),
                  pl.BlockSpec((1, SSD_HEADS, 1), lambda d, b, c: (d, 0, 0))],
        out_specs=pl.BlockSpec((1, 1, q, SSD_WIDTH), lambda d, b, c: (d, b, pos(d, c), 0)),
        out_shape=jax.ShapeDtypeStruct((2, bsz, ln, SSD_WIDTH), F32),
        scratch_shapes=[pltpu.VMEM((SSD_GROUPS, SSD_STATE, SSD_GW), F32)],
        compiler_params=_cparams(("parallel", "parallel", "arbitrary")),
        name="ssd_scan",
    )(xbc, xbc, xbc, dt, dtt, a_row, a_col)


EO_ROWS = 16


def _even_out_kernel(tq, yf_ref, yb_ref, xs_ref, zs_ref, cu_ref, dv_ref, ng_ref, lg_ref, lb_ref, o_ref):
    gw = SSD_WIDTH // SSD_GROUPS

    def body(r, carry):
        rows = pl.ds(pl.multiple_of(r * EO_ROWS, EO_ROWS), EO_ROWS)
        y = yf_ref[0, 0, rows, :] + yb_ref[0, 0, rows, :]
        y = y + dv_ref[...] * xs_ref[0, rows, :]
        y = y * zs_ref[rows, :]
        for g in range(SSD_GROUPS):
            sl = slice(g * gw, (g + 1) * gw)
            yg = y[:, sl]
            ms = jnp.mean(yg * yg, axis=-1, keepdims=True)
            o_ref[rows, sl] = (yg * lax.rsqrt(ms + EPS) * ng_ref[:, sl]).astype(o_ref.dtype)
        cv = cu_ref[0, rows, :]
        mu = jnp.mean(cv, axis=-1, keepdims=True)
        xc = cv - mu
        var = jnp.mean(xc * xc, axis=-1, keepdims=True)
        ln = xc * lax.rsqrt(var + EPS) * lg_ref[...] + lb_ref[...]
        o_ref[rows, SSD_WIDTH:SSD_WIDTH + CM_WIDTH] = _silu(ln).astype(o_ref.dtype)
        return carry
    lax.fori_loop(0, tq // EO_ROWS, body, 0, unroll=2)


def even_out(y, xbc, zs, cu, dvec, ng, lg, lb, *, blk0, nblk, tq):
    bsz = xbc.shape[0]
    vec = pl.BlockSpec((1, SSD_WIDTH), lambda b, i: (0, 0))
    return pl.pallas_call(
        functools.partial(_even_out_kernel, tq),
        grid=(bsz, nblk),
        in_specs=[pl.BlockSpec((1, 1, tq, SSD_WIDTH), lambda b, i: (0, b, blk0 + i, 0)),
                  pl.BlockSpec((1, 1, tq, SSD_WIDTH), lambda b, i: (1, b, blk0 + i, 0)),
                  pl.BlockSpec((1, tq, SSD_WIDTH), lambda b, i: (b, blk0 + i, 0)),
                  pl.BlockSpec((tq, SSD_WIDTH), lambda b, i: (b * nblk + i, 0)),
                  pl.BlockSpec((1, tq, CM_WIDTH), lambda b, i: (b, blk0 + i, 0)),
                  vec, vec, vec, vec],
        out_specs=pl.BlockSpec((tq, SSD_WIDTH + CM_WIDTH), lambda b, i: (b * nblk + i, 0)),
        out_shape=jax.ShapeDtypeStruct((bsz * nblk * tq, SSD_WIDTH + CM_WIDTH), BF16),
        compiler_params=_cparams(("parallel", "parallel")),
        name="even_mixer_tail",
    )(y, y, xbc, zs, cu, dvec, ng, lg, lb)


_NT = (((1,), (1,)), ((), ()))


def _na_kernel(rows_total, q_ref, k_ref, v_ref, kc_ref, vc_ref, bias_ref, o_ref):
    rb = pl.program_id(2)
    k0 = jnp.clip(rb * NA_QROWS - NA_ROWS // 2, 0, rows_total - NA_KROWS) * GRID_W
    k0 = pl.multiple_of(k0, GRID_W)
    nk = NA_KROWS * GRID_W
    for h in range(NA_HPS):
        hc = slice(h * NA_HEAD_DIM, (h + 1) * NA_HEAD_DIM)
        qv = q_ref[:, hc]
        s = lax.dot_general(qv, k_ref[pl.ds(k0, nk), hc], _NT, preferred_element_type=F32) + bias_ref[0, h]
        sc = lax.dot_general(qv, kc_ref[:, hc], _NT, preferred_element_type=F32)
        m = jnp.maximum(jnp.max(s, axis=-1, keepdims=True), jnp.max(sc, axis=-1, keepdims=True))
        p = jnp.exp(s - m)
        pc = jnp.exp(sc - m)
        den = jnp.sum(p, axis=-1, keepdims=True) + jnp.sum(pc, axis=-1, keepdims=True)
        o = jnp.dot(p.astype(BF16), v_ref[pl.ds(k0, nk), hc], preferred_element_type=F32)
        o = o + jnp.dot(pc.astype(BF16), vc_ref[:, hc], preferred_element_type=F32)
        o_ref[:, hc] = (o / den).astype(o_ref.dtype)


def neighbourhood_attention(q, k, v, kc, vc, bias, bsz, seq, n_ctx):
    rows = seq // GRID_W
    nrb = rows // NA_QROWS
    tq = NA_QROWS * GRID_W
    nk = NA_KROWS * GRID_W
    hw = NA_HPS * NA_HEAD_DIM

    def btype(rb):
        return jnp.where(rb == 0, 0, jnp.where(rb == nrb - 1, 2, 1))

    return pl.pallas_call(
        functools.partial(_na_kernel, rows),
        grid=(NA_HEADS // NA_HPS, bsz, nrb),
        in_specs=[pl.BlockSpec((tq, hw), lambda h, b, r: (b * nrb + r, h)),
                  pl.BlockSpec((seq, hw), lambda h, b, r: (b, h)),
                  pl.BlockSpec((seq, hw), lambda h, b, r: (b, h)),
                  pl.BlockSpec((n_ctx, hw), lambda h, b, r: (b, h)),
                  pl.BlockSpec((n_ctx, hw), lambda h, b, r: (b, h)),
                  pl.BlockSpec((1, NA_HPS, tq, nk), lambda h, b, r: (btype(r), h, 0, 0))],
        out_specs=pl.BlockSpec((tq, hw), lambda h, b, r: (b * nrb + r, h)),
        out_shape=jax.ShapeDtypeStruct((bsz * seq, NA_WIDTH), BF16),
        compiler_params=_cparams(("parallel", "parallel", "arbitrary")),
        name="neighbourhood_attention",
    )(q, k, v, kc, vc, bias)


def _ctx_attn_kernel(q_ref, k_ref, v_ref, o_ref):
    s = lax.dot_general(q_ref[...], k_ref[...], _NT, preferred_element_type=F32)
    m = jnp.max(s, axis=-1, keepdims=True)
    p = jnp.exp(s - m)
    den = jnp.sum(p, axis=-1, keepdims=True)
    o = jnp.dot(p.astype(BF16), v_ref[...], preferred_element_type=F32)
    o_ref[...] = (o / den).astype(o_ref.dtype)


def context_attention(q, k, v, bsz, n_ctx):
    spec = pl.BlockSpec((n_ctx, NA_HEAD_DIM), lambda b, h: (b, h))
    return pl.pallas_call(
        _ctx_attn_kernel,
        grid=(bsz, NA_HEADS),
        in_specs=[spec, spec, spec],
        out_specs=spec,
        out_shape=jax.ShapeDtypeStruct((bsz * n_ctx, NA_WIDTH), BF16),
        compiler_params=_cparams(("parallel", "parallel")),
        name="context_attention",
    )(q, k, v)


def _na_bias_tables(rpb, rows):
    nrb = rows // NA_QROWS
    col = np.arange(GRID_W)
    c0 = np.clip(col - NA_COLS // 2, 0, GRID_W - NA_COLS)
    col_ok = (col[None, :] >= c0[:, None]) & (col[None, :] < c0[:, None] + NA_COLS)
    col_idx = np.clip(col[None, :] - col[:, None] + NA_COLS - 1, 0, 2 * NA_COLS - 2)
    kr = min(NA_ROWS, rows)
    ridx = np.zeros((3, NA_QROWS, NA_KROWS), np.int32)
    rok = np.zeros((3, NA_QROWS, NA_KROWS), bool)
    for t, rb in enumerate((0, 1, nrb - 1)):
        k0 = int(np.clip(rb * NA_QROWS - NA_ROWS // 2, 0, rows - NA_KROWS))
        for rq in range(NA_QROWS):
            r = rb * NA_QROWS + rq
            r0 = int(np.clip(r - kr // 2, 0, rows - kr))
            for rk in range(NA_KROWS):
                ra = k0 + rk
                rok[t, rq, rk] = r0 <= ra < r0 + kr
                ridx[t, rq, rk] = int(np.clip(ra - r + NA_ROWS - 1, 0, 2 * NA_ROWS - 2))
    n_off = 2 * NA_ROWS - 1
    tz = rpb[:, :, col_idx].astype(F32)
    tz = jnp.where(col_ok[None, None], tz, NEG_BIG)
    tz = jnp.concatenate([tz, jnp.full((NA_HEADS, 1, GRID_W, GRID_W), NEG_BIG, F32)], axis=1)
    tile_of = np.where(rok, ridx, n_off)

    def build(tz_ref, o_ref):
        for t in range(3):
            @pl.when(pl.program_id(0) == t)
            def _(t=t):
                for rq in range(NA_QROWS):
                    for pr in range(NA_KROWS // 2):
                        a, b = int(tile_of[t, rq, 2 * pr]), int(tile_of[t, rq, 2 * pr + 1])
                        o_ref[0, 0, rq * GRID_W:(rq + 1) * GRID_W, 2 * pr * GRID_W:(2 * pr + 2) * GRID_W] = (
                            jnp.concatenate([tz_ref[0, a], tz_ref[0, b]], axis=1))

    return pl.pallas_call(
        build,
        grid=(3, NA_HEADS),
        in_specs=[pl.BlockSpec((1, n_off + 1, GRID_W, GRID_W), lambda t, h: (h, 0, 0, 0))],
        out_specs=pl.BlockSpec((1, 1, NA_QROWS * GRID_W, NA_KROWS * GRID_W), lambda t, h: (t, h, 0, 0)),
        out_shape=jax.ShapeDtypeStruct((3, NA_HEADS, NA_QROWS * GRID_W, NA_KROWS * GRID_W), F32),
        compiler_params=_cparams(("parallel", "parallel")),
        name="na_bias_tables",
    )(tz)


class _Stream:
    def __init__(self, mod, rows_per_mod, tm):
        self.mod = mod
        self.rows_per_mod = rows_per_mod
        self.tm = tm

    def m(self, idx):
        return self.mod[:, idx][:, None, :]


def _ffn_half(h, st, g, w_in, w_out, j):
    xn = adaln(h, st.m(3 * j), st.m(3 * j + 1), g, rows_per_mod=st.rows_per_mod)
    hid = proj_matmul(xn, [w_in, w_in], [0, FFN_DIM], FFN_DIM, _ep_swiglu, BF16, tm=min(2 * st.tm, h.shape[0]),
                      tn=512, name="ffn_in")
    return matmul_residual([hid], [w_out + (0, FFN_DIM)], [0], h, st.m(3 * j + 2), 0.5,
                           rows_per_mod=st.rows_per_mod, tm=st.tm, tn=512, name="ffn_out")


def _proj(xn, st, w, names, n_out, ep, dtype, extras=(), name="mixer_in"):
    return proj_matmul(xn, [w[n][0] for n in names], [w[n][1] for n in names], n_out, ep, dtype,
                       tm=st.tm, tn=w[names[0]][2], extras=extras, w_transposed=w["transposed"], name=name)


def _mixer_even(x, cx, sx, sc, g, p, bsz, seq, n_ctx):
    w = p["w_in"]
    outs = {}
    for key, h, st in (("x", x, sx), ("c", cx, sc)):
        xn = adaln(h, st.m(3), st.m(4), g, rows_per_mod=st.rows_per_mod)
        zs = _proj(xn, st, w, ("z",), SSD_WIDTH, _ep_silu, F32, name="ssd_z")
        xbc = _proj(xn, st, w, ("xbc",), SSD_XBC, _ep_id, F32, name="ssd_xbc")
        dt = _proj(xn, st, w, ("dt",), 2 * SSD_HEADS, _ep_softplus_bias, F32, extras=(p["dt_bias"],), name="ssd_dt")
        u = _proj(xn, st, w, ("ga", "gg"), CM_WIDTH, _ep_glu_sigmoid, F32, name="conformer_glu")
        outs[key] = (zs, xbc, dt, u)
    zs_x, xbc_x, dt_x, u_x = outs["x"]
    zs_c, xbc_c, dt_c, u_c = outs["c"]
    xbc = conv3_silu_joint(xbc_c.reshape(bsz, n_ctx, SSD_XBC), xbc_x.reshape(bsz, seq, SSD_XBC),
                           p["conv_w"], p["conv_b"])
    dt = jnp.concatenate([dt_c.reshape(bsz, n_ctx, 2, SSD_HEADS), dt_x.reshape(bsz, seq, 2, SSD_HEADS)], axis=1)
    dt = jnp.transpose(dt, (2, 0, 1, 3))
    dtt = jnp.transpose(dt, (0, 1, 3, 2))
    y = ssd_scan_joint(xbc, dt, dtt, p["a_row"], p["a_col"], n_ctx // SSD_CHUNK)
    cu = conv31_joint(u_c.reshape(bsz, n_ctx, CM_WIDTH), u_x.reshape(bsz, seq, CM_WIDTH), p["cm_w"], p["cm_b"])
    tq = n_ctx
    tail = functools.partial(even_out, y, xbc, dvec=p["dvec"], ng=p["norm_g"], lg=p["ln_g"], lb=p["ln_b"], tq=tq)
    lhs_x = tail(zs=zs_x, cu=cu, blk0=1, nblk=seq // tq)
    lhs_c = tail(zs=zs_c, cu=cu, blk0=0, nblk=1)
    return lhs_x, lhs_c


def _mixer_odd(x, cx, sx, sc, g, p, bsz, seq, n_ctx, need_ctx):
    w = p["w_in"]
    hg_q, hg_k = p["q_g"], p["k_g"]
    ep_q = functools.partial(_ep_head_rms, NA_HEAD_DIM ** -0.5)
    ep_k = functools.partial(_ep_head_rms, None)
    xn_x = adaln(x, sx.m(3), sx.m(4), g, rows_per_mod=sx.rows_per_mod)
    xn_c = adaln(cx, sc.m(3), sc.m(4), g, rows_per_mod=sc.rows_per_mod)
    kc = _proj(xn_c, sc, w, ("k",), NA_WIDTH, ep_k, BF16, extras=(hg_k,), name="na_k")
    vc = _proj(xn_c, sc, w, ("v",), NA_WIDTH, _ep_id, BF16, name="na_v")
    q = _proj(xn_x, sx, w, ("q",), NA_WIDTH, ep_q, BF16, extras=(hg_q,), name="na_q")
    k = _proj(xn_x, sx, w, ("k",), NA_WIDTH, ep_k, BF16, extras=(hg_k,), name="na_k")
    v = _proj(xn_x, sx, w, ("v",), NA_WIDTH, _ep_id, BF16, name="na_v")
    gb = _proj(xn_x, sx, w, ("gb",), SC_WIDTH, _ep_id, F32, name="sc_gate")
    pr = _proj(xn_x, sx, w, ("gc", "hs"), SC_WIDTH, _ep_mul, F32, name="sc_prod")
    ox = neighbourhood_attention(q, k, v, kc, vc, p["bias"], bsz, seq, n_ctx)
    yx = gated_conv3(pr.reshape(bsz, seq, SC_WIDTH), gb.reshape(bsz, seq, SC_WIDTH), p["sc_w"])
    yx = yx.reshape(bsz * seq, SC_WIDTH)
    if not need_ctx:
        return (ox, yx), None
    qc = _proj(xn_c, sc, w, ("q",), NA_WIDTH, ep_q, BF16, extras=(hg_q,), name="na_q")
    gbc = _proj(xn_c, sc, w, ("gb",), SC_WIDTH, _ep_id, F32, name="sc_gate")
    prc = _proj(xn_c, sc, w, ("gc", "hs"), SC_WIDTH, _ep_mul, F32, name="sc_prod")
    oc = context_attention(qc, kc, vc, bsz, n_ctx)
    yc = gated_conv3(prc.reshape(bsz, n_ctx, SC_WIDTH), gbc.reshape(bsz, n_ctx, SC_WIDTH), p["sc_w"])
    yc = yc.reshape(bsz * n_ctx, SC_WIDTH)
    return (ox, yx), (oc, yc)


def _segments(w, prefix, names, sizes, tiles, transposed=False):
    out, acc = {"transposed": transposed}, 0
    for n, s, tn in zip(names, sizes, tiles):
        if acc % tn == 0 and (transposed or s % LANES == 0):
            out[n] = ((w, prefix), acc, tn)
        elif transposed:
            out[n] = ((w[prefix + (slice(acc, acc + s),)], ()), 0, tn)
        else:
            out[n] = ((w[prefix][:, acc:acc + s], ()), 0, tn)
        acc += s
    return out


def kernel(x, c, ctx, c_ctx, w_mod, b_mod, norm_g, ffn_w_in, ffn_w_out, ev_w_in, ev_w_out, ssd_conv_w, ssd_conv_b,
           ssd_dt_bias, ssd_a_log, ssd_d, ssd_norm_g, cm_conv_w, cm_conv_b, cm_ln_g, cm_ln_b, od_w_in, od_w_out,
           na_q_g, na_k_g, na_rpb, sc_conv_w):
    bsz, seq, d = x.shape
    n_ctx = ctx.shape[1]
    rows = seq // GRID_W

    cvec = jnp.concatenate([c, c_ctx[None, :], jnp.zeros((8 - bsz - 1, d), F32)], axis=0)
    mod = modulation(cvec, w_mod, b_mod).reshape(DEPTH, 8, N_MOD, d)

    ffn_out_b = ffn_w_out.astype(BF16)
    ev_out_b = ev_w_out.astype(BF16)
    od_out_b = od_w_out.astype(BF16)
    ev_w_in_t = jnp.swapaxes(ev_w_in, 1, 2)

    xs = x.reshape(bsz * seq, d)
    cs = ctx.reshape(bsz * n_ctx, d)
    for i in range(DEPTH):
        last = i == DEPTH - 1
        sx = _Stream(mod[i, :bsz], seq, 1024)
        sc = _Stream(mod[i, bsz:bsz + 1], bsz * n_ctx, bsz * n_ctx)
        g = norm_g[i]
        xs = _ffn_half(xs, sx, g[0:1], (ffn_w_in, (i, 0)), (ffn_out_b, (i,0)), 0)
        cs = _ffn_half(cs, sc, g[0:1], (ffn_w_in, (i, 0)), (ffn_out_b, (i,0)), 0)
        j = i // 2
        if i % 2 == 0:
            names = ("z", "xbc", "dt", "ga", "gg")
            sizes = (SSD_WIDTH, SSD_XBC, 2 * SSD_HEADS, CM_WIDTH, CM_WIDTH)
            a_neg = -jnp.exp(ssd_a_log[j].astype(F32))
            p = {
                "w_in": _segments(ev_w_in_t, (j,), names, sizes, (1024, 1024, 2 * SSD_HEADS, 512, 512), True),
                "dt_bias": ssd_dt_bias[j].reshape(1, 2 * SSD_HEADS).astype(F32),
                "conv_w": ssd_conv_w[j], "conv_b": ssd_conv_b[j][None, :],
                "a_row": a_neg[:, None, :], "a_col": a_neg[:, :, None],
                "cm_w": cm_conv_w[j], "cm_b": cm_conv_b[j][None, :],
                "dvec": jnp.repeat(ssd_d[j], SSD_HEAD_DIM)[None, :],
                "norm_g": ssd_norm_g[j][None, :], "ln_g": cm_ln_g[j][None, :], "ln_b": cm_ln_b[j][None, :],
            }
            lhs_x, lhs_c = _mixer_even(xs, cs, sx, sc, g[1:2], p, bsz, seq, n_ctx)
            w_o = [(ev_out_b, (j,), 0, SSD_WIDTH + CM_WIDTH)]
            xs = matmul_residual([lhs_x], w_o, [0], xs, sx.m(5), 1.0,
                                 rows_per_mod=sx.rows_per_mod, tm=sx.tm, tn=512, name="mixer_out")
            cs_mix = [lhs_c], w_o, [0]
        else:
            names = ("q", "k", "v", "gb", "gc", "hs")
            p = {
                "w_in": _segments(od_w_in, (j,), names, (NA_WIDTH,) * 3 + (SC_WIDTH,) * 3, (1024,) * 4 + (512,) * 2),
                "q_g": na_q_g[j][None, :], "k_g": na_k_g[j][None, :],
                "bias": _na_bias_tables(na_rpb[j], rows), "sc_w": sc_conv_w[j],
            }
            lx, lc = _mixer_odd(xs, cs, sx, sc, g[1:2], p, bsz, seq, n_ctx, not last)
            w_o = [(od_out_b, (j,), 0, NA_WIDTH), (od_out_b, (j,), 1, SC_WIDTH)]
            xs = matmul_residual(list(lx), w_o, [0, 0], xs, sx.m(5), 1.0,
                                 rows_per_mod=sx.rows_per_mod, tm=sx.tm, tn=512, name="mixer_out")
            cs_mix = (list(lc), w_o, [0, 0]) if lc is not None else None
        xs = _ffn_half(xs, sx, g[2:3], (ffn_w_in, (i, 1)), (ffn_out_b, (i,1)), 2)
        if not last:
            cs = matmul_residual(cs_mix[0], cs_mix[1], cs_mix[2], cs, sc.m(5), 1.0,
                                 rows_per_mod=sc.rows_per_mod, tm=sc.tm, tn=512, name="mixer_out")
            cs = _ffn_half(cs, sc, g[2:3], (ffn_w_in, (i, 1)), (ffn_out_b, (i,1)), 2)
    return xs.reshape(bsz, seq, d)
```

```python
import functools

import numpy as np
import jax
import jax.numpy as jnp
from jax import lax
from jax.experimental import pallas as pl
from jax.experimental.pallas import tpu as pltpu

F32 = jnp.float32
BF16 = jnp.bfloat16

D_MODEL = 2048
DEPTH = 4
GRID_W = 64
N_MOD = 9
FFN_DIM = 5632
SSD_HEADS = 32
SSD_HEAD_DIM = 64
SSD_WIDTH = SSD_HEADS * SSD_HEAD_DIM
SSD_GROUPS = 8
SSD_STATE = 128
SSD_CHUNK = 128
SSD_GN = SSD_GROUPS * SSD_STATE
SSD_XBC = SSD_WIDTH + 2 * SSD_GN
SSD_HPG = SSD_HEADS // SSD_GROUPS
SSD_GW = SSD_HPG * SSD_HEAD_DIM
CM_WIDTH = 2048
CM_CONV = 31
NA_HEADS = 16
NA_HEAD_DIM = 128
NA_WIDTH = NA_HEADS * NA_HEAD_DIM
NA_ROWS = 8
NA_COLS = 16
SC_WIDTH = 2048
EPS = 1e-6

NA_QROWS = 8
NA_KROWS = 16
NA_HPS = 4
NEG_BIG = -1e30

VMEM_LIMIT = 56 * 1024 * 1024


def _cparams(sem):
    return pltpu.CompilerParams(dimension_semantics=sem, vmem_limit_bytes=VMEM_LIMIT)


def _silu(v):
    return v * jax.nn.sigmoid(v)


def _mod_kernel(c_ref, w_ref, b_ref, o_ref):
    s = _silu(c_ref[...]).astype(BF16)
    o_ref[0] = jnp.dot(s, w_ref[0].astype(BF16), preferred_element_type=F32) + b_ref[0]


def modulation(cvec, w_mod, b_mod, tn=1024):
    depth, d, n = w_mod.shape
    rows = cvec.shape[0]
    return pl.pallas_call(
        _mod_kernel,
        grid=(depth, n // tn),
        in_specs=[pl.BlockSpec((rows, d), lambda l, j: (0, 0)),
                  pl.BlockSpec((1, d, tn), lambda l, j: (l, 0, j)),
                  pl.BlockSpec((1, 1, tn), lambda l, j: (l, 0, j))],
        out_specs=pl.BlockSpec((1, rows, tn), lambda l, j: (l, 0, j)),
        out_shape=jax.ShapeDtypeStruct((depth, rows, n), F32),
        compiler_params=_cparams(("parallel", "parallel")),
        name="modulation",
    )(cvec, w_mod, b_mod.reshape(depth, 1, n))


NORM_ROWS = 16
NORM_UNROLL = 4


def _adaln_kernel(tm, x_ref, sh_ref, sc_ref, g_ref, o_ref):
    def body(r, carry):
        rows = pl.ds(pl.multiple_of(r * NORM_ROWS, NORM_ROWS), NORM_ROWS)
        xv = x_ref[rows, :]
        ms = jnp.mean(xv * xv, axis=-1, keepdims=True)
        y = xv * lax.rsqrt(ms + EPS) * g_ref[...]
        y = y * (1.0 + sc_ref[0]) + sh_ref[0]
        o_ref[rows, :] = y.astype(o_ref.dtype)
        return carry
    lax.fori_loop(0, tm // NORM_ROWS, body, 0, unroll=NORM_UNROLL)


def adaln(x, shift, scale, g, *, rows_per_mod):
    t, d = x.shape
    tm = min(1024, rows_per_mod)
    assert t % tm == 0 and rows_per_mod % tm == 0
    mod_spec = pl.BlockSpec((1, 1, d), lambda i: ((i * tm) // rows_per_mod, 0, 0))
    return pl.pallas_call(
        functools.partial(_adaln_kernel, tm),
        grid=(t // tm,),
        in_specs=[pl.BlockSpec((tm, d), lambda i: (i, 0)), mod_spec, mod_spec, pl.BlockSpec((1, d), lambda i: (0, 0))],
        out_specs=pl.BlockSpec((tm, d), lambda i: (i, 0)),
        out_shape=jax.ShapeDtypeStruct((t, d), BF16),
        compiler_params=_cparams(("parallel",)),
        name="adaln",
    )(x, shift, scale, g)


def _pm_kernel(n_w, n_extra, epilogue, w_transposed, tiles, *refs):
    n_s = len(tiles)
    xn_refs = refs[:n_s]
    w_refs = refs[n_s:n_s + n_w]
    extra = refs[n_s + n_w:n_s + n_w + n_extra]
    o_refs = refs[n_s + n_w + n_extra:2 * n_s + n_w + n_extra]
    wb_refs = refs[2 * n_s + n_w + n_extra:]
    i = pl.program_id(1)

    @pl.when(i == 0)
    def _():
        for w, wb in zip(w_refs, wb_refs):
            wv = w[...]
            wb[...] = (wv.T if w_transposed else wv).astype(BF16)

    lo = 0
    for xn_ref, o_ref, nt in zip(xn_refs, o_refs, tiles):
        def run(xn_ref=xn_ref, o_ref=o_ref):
            xn = xn_ref[...]
            accs = [jnp.dot(xn, wb[...], preferred_element_type=F32) for wb in wb_refs]
            epilogue(accs, extra, o_ref)
        if n_s == 1:
            run()
        else:
            pl.when((i >= lo) & (i < lo + nt))(run)
        lo += nt


def proj_matmul(xns, ws, col_offs, n_out, epilogue, out_dtype, *, tms, tn, extras=(), w_transposed=False,
                name="proj_matmul"):
    d = xns[0].shape[1]
    assert n_out % tn == 0
    for off in col_offs:
        assert off % tn == 0
    tiles, los, lo = [], [], 0
    for xn, tm in zip(xns, tms):
        assert xn.shape[0] % tm == 0
        tiles.append(xn.shape[0] // tm)
        los.append(lo)
        lo += tiles[-1]

    def row_tile(i, s):
        return jnp.clip(i - los[s], 0, tiles[s] - 1)

    in_specs = [pl.BlockSpec((tm, d), functools.partial(lambda j, i, s: (row_tile(i, s), 0), s=s))
                for s, tm in enumerate(tms)]
    for (_, prefix), off in zip(ws, col_offs):
        if w_transposed:
            in_specs.append(pl.BlockSpec((None,) * len(prefix) + (tn, d),
                                         functools.partial(lambda j, i, p, o: p + (j + o, 0), p=prefix, o=off // tn)))
        else:
            in_specs.append(pl.BlockSpec((None,) * len(prefix) + (d, tn),
                                         functools.partial(lambda j, i, p, o: p + (0, j + o), p=prefix, o=off // tn)))
    for e in extras:
        if e.shape[1] == n_out:
            in_specs.append(pl.BlockSpec((1, tn), lambda j, i: (0, j)))
        else:
            in_specs.append(pl.BlockSpec(e.shape, lambda j, i: (0, 0)))
    return pl.pallas_call(
        functools.partial(_pm_kernel, len(ws), len(extras), epilogue, w_transposed, tuple(tiles)),
        grid=(n_out // tn, lo),
        in_specs=in_specs,
        out_specs=[pl.BlockSpec((tm, tn), functools.partial(lambda j, i, s: (row_tile(i, s), j), s=s))
                   for s, tm in enumerate(tms)],
        out_shape=[jax.ShapeDtypeStruct((xn.shape[0], n_out), out_dtype) for xn in xns],
        scratch_shapes=[pltpu.VMEM((d, tn), BF16) for _ in ws],
        compiler_params=_cparams(("arbitrary", "arbitrary")),
        name=name,
    )(*xns, *[w for w, _ in ws], *extras)


def _ep_swiglu(accs, extra, o_ref):
    a, g = accs
    o_ref[...] = (_silu(a) * g).astype(o_ref.dtype)


def _ep_glu_sigmoid(accs, extra, o_ref):
    a, g = accs
    o_ref[...] = (a * jax.nn.sigmoid(g)).astype(o_ref.dtype)


def _ep_mul(accs, extra, o_ref):
    a, g = accs
    o_ref[...] = (a * g).astype(o_ref.dtype)


def _ep_silu(accs, extra, o_ref):
    o_ref[...] = _silu(accs[0]).astype(o_ref.dtype)


def _ep_id(accs, extra, o_ref):
    o_ref[...] = accs[0].astype(o_ref.dtype)


def _ep_softplus_bias(accs, extra, o_ref):
    v = accs[0] + extra[0][...]
    o_ref[...] = jnp.maximum(v, 0.0) + jnp.log1p(jnp.exp(-jnp.abs(v)))


def _ep_head_rms(post_scale, accs, extra, o_ref):
    a = accs[0]
    hg = extra[0][...]
    for h in range(a.shape[1] // NA_HEAD_DIM):
        sl = slice(h * NA_HEAD_DIM, (h + 1) * NA_HEAD_DIM)
        ah = a[:, sl]
        ms = jnp.mean(ah * ah, axis=-1, keepdims=True)
        y = ah * lax.rsqrt(ms + EPS) * hg
        if post_scale is not None:
            y = y * post_scale
        o_ref[:, sl] = y.astype(o_ref.dtype)


def _mr_kernel(n_p, coef, *refs):
    lhs = refs[:n_p]
    ws = refs[n_p:2 * n_p]
    x_ref, gate_ref, o_ref = refs[2 * n_p:]
    acc = jnp.dot(lhs[0][...], ws[0][...].astype(BF16), preferred_element_type=F32)
    for p in range(1, n_p):
        acc = acc + jnp.dot(lhs[p][...], ws[p][...].astype(BF16), preferred_element_type=F32)
    gate = gate_ref[0]
    if coef != 1.0:
        gate = coef * gate
    o_ref[...] = x_ref[...] + gate * acc


def matmul_residual(lhs_list, w_list, lhs_offs, x, gate, coef, *, rows_per_mod, tm, tn, name="matmul_residual"):
    t, d = x.shape
    assert t % tm == 0 and d % tn == 0 and rows_per_mod % tm == 0
    in_specs = []
    for l, (_, _, _, kp), off in zip(lhs_list, w_list, lhs_offs):
        assert off % kp == 0
        in_specs.append(pl.BlockSpec((tm, kp), functools.partial(lambda i, j, o: (i, o), o=off // kp)))
    for _, prefix, rblk, kp in w_list:
        in_specs.append(pl.BlockSpec((None,) * len(prefix) + (kp, tn),
                                     functools.partial(lambda i, j, p, r: p + (r, j), p=prefix, r=rblk)))
    w_list = [w for w, _, _, _ in w_list]
    in_specs.append(pl.BlockSpec((tm, tn), lambda i, j: (i, j)))
    in_specs.append(pl.BlockSpec((1, 1, tn), lambda i, j: ((i * tm) // rows_per_mod, 0, j)))
    return pl.pallas_call(
        functools.partial(_mr_kernel, len(lhs_list), coef),
        grid=(t // tm, d // tn),
        in_specs=in_specs,
        out_specs=pl.BlockSpec((tm, tn), lambda i, j: (i, j)),
        out_shape=jax.ShapeDtypeStruct((t, d), F32),
        compiler_params=_cparams(("parallel", "parallel")),
        name=name,
    )(*lhs_list, *w_list, x, gate)


def _shift_rows(v, s):
    n = v.shape[0]
    r = pltpu.roll(v, s % n, axis=0)
    row = lax.broadcasted_iota(jnp.int32, v.shape, 0)
    if s > 0:
        return jnp.where(row < s, 0.0, r)
    return jnp.where(row >= n + s, 0.0, r)


def _conv3(v, w_ref):
    return (w_ref[0:1, :] * _shift_rows(v, 1) + w_ref[1:2, :] * v) + w_ref[2:3, :] * _shift_rows(v, -1)


def _conv3_silu_kernel(n_c, xc_ref, xx_ref, w_ref, b_ref, o_ref):
    for src, lo, ln in ((xc_ref, 0, n_c), (xx_ref, n_c, xx_ref.shape[1])):
        y = _conv3(src[0], w_ref) + b_ref[...]
        o_ref[0, lo:lo + ln, :] = _silu(y)


def conv3_silu_joint(xc, xx, w, b, ct=256):
    bsz, lc, ch = xc.shape
    lx = xx.shape[1]
    return pl.pallas_call(
        functools.partial(_conv3_silu_kernel, lc),
        grid=(bsz, ch // ct),
        in_specs=[pl.BlockSpec((1, lc, ct), lambda b_, j: (b_, 0, j)),
                  pl.BlockSpec((1, lx, ct), lambda b_, j: (b_, 0, j)),
                  pl.BlockSpec((3, ct), lambda b_, j: (0, j)),
                  pl.BlockSpec((1, ct), lambda b_, j: (0, j))],
        out_specs=pl.BlockSpec((1, lc + lx, ct), lambda b_, j: (b_, 0, j)),
        out_shape=jax.ShapeDtypeStruct((bsz, lc + lx, ch), F32),
        compiler_params=_cparams(("parallel", "parallel")),
        name="ssd_conv3_silu",
    )(xc, xx, w, b)


def _gated_conv3_kernel(p_ref, gate_ref, w_ref, o_ref):
    o_ref[0] = (gate_ref[0] * _conv3(p_ref[0], w_ref)).astype(o_ref.dtype)


def gated_conv3(prod, gate, w, ct=256):
    bsz, ln, ch = prod.shape
    spec = pl.BlockSpec((1, ln, ct), lambda b_, j: (b_, 0, j))
    return pl.pallas_call(
        _gated_conv3_kernel,
        grid=(bsz, ch // ct),
        in_specs=[spec, spec, pl.BlockSpec((3, ct), lambda b_, j: (0, j))],
        out_specs=spec,
        out_shape=jax.ShapeDtypeStruct((bsz, ln, ch), BF16),
        compiler_params=_cparams(("parallel", "parallel")),
        name="short_gated_conv",
    )(prod, gate, w)


CONV_HALO = 16
CONV_ROWS = 128
SUBLANES = 8
LANES = 128


def _conv31_kernel(n_c, uc_ref, ux_ref, w_ref, b_ref, o_ref, sh_ref):
    ct = o_ref.shape[2]

    def run(src_ref, lo, ln):
        lp = ln + 2 * CONV_HALO
        zeros = jnp.zeros((CONV_HALO, ct), F32)
        sh_ref[0, 0:CONV_HALO, :] = zeros
        sh_ref[0, CONV_HALO:CONV_HALO + ln, :] = src_ref[0]
        sh_ref[0, CONV_HALO + ln:lp, :] = zeros
        p = sh_ref[0, 0:lp, :]
        for s in range(1, SUBLANES):
            sh_ref[s, 0:lp, :] = pltpu.roll(p, lp - s, axis=0)

        def body(ci, carry):
            r0 = pl.multiple_of(ci * CONV_ROWS, CONV_ROWS)
            acc = jnp.zeros((CONV_ROWS, ct), F32) + b_ref[...]
            for k in range(CM_CONV):
                a, s = divmod(k + CONV_HALO - (CM_CONV - 1) // 2, SUBLANES)
                start = pl.multiple_of(r0 + SUBLANES * a, SUBLANES)
                acc = acc + sh_ref[s, pl.ds(start, CONV_ROWS), :] * w_ref[k:k + 1, :]
            o_ref[0, pl.ds(pl.multiple_of(lo + r0, CONV_ROWS), CONV_ROWS), :] = acc
            return carry
        lax.fori_loop(0, ln // CONV_ROWS, body, 0)

    run(uc_ref, 0, n_c)
    run(ux_ref, n_c, ux_ref.shape[1])


def conv31_joint(uc, ux, w, b, ct=128):
    bsz, lc, ch = uc.shape
    lx = ux.shape[1]
    return pl.pallas_call(
        functools.partial(_conv31_kernel, lc),
        grid=(bsz, ch // ct),
        in_specs=[pl.BlockSpec((1, lc, ct), lambda b_, j: (b_, 0, j)),
                  pl.BlockSpec((1, lx, ct), lambda b_, j: (b_, 0, j)),
                  pl.BlockSpec((CM_CONV, ct), lambda b_, j: (0, j)),
                  pl.BlockSpec((1, ct), lambda b_, j: (0, j))],
        out_specs=pl.BlockSpec((1, lc + lx, ct), lambda b_, j: (b_, 0, j)),
        out_shape=jax.ShapeDtypeStruct((bsz, lc + lx, ch), F32),
        scratch_shapes=[pltpu.VMEM((SUBLANES, lx + 2 * CONV_HALO, ct), F32)],
        compiler_params=_cparams(("parallel", "parallel")),
        name="conformer_conv31",
    )(uc, ux, w, b)


def _ssd_kernel(xs_ref, b_ref, c_ref, dt_ref, dtt_ref, ar_ref, ac_ref, y_ref, state_ref):
    q = SSD_CHUNK
    sgn = 1 - 2 * pl.program_id(0)

    @pl.when(pl.program_id(2) == 0)
    def _():
        state_ref[...] = jnp.zeros(state_ref.shape, F32)

    dtc = dt_ref[0, 0]
    dtr = dtt_ref[0, 0]
    da_c = dtc * ar_ref[0]
    da_r = dtr * ac_ref[0]
    ri = lax.broadcasted_iota(jnp.int32, (q, q), 0)
    ci = lax.broadcasted_iota(jnp.int32, (q, q), 1)
    keep = (ci - ri) * sgn <= 0
    tri_c = jnp.where(keep, 1.0, 0.0)
    tri_r = jnp.where((ri - ci) * sgn <= 0, 1.0, 0.0)
    acum_c = jnp.dot(tri_c, da_c, preferred_element_type=F32, precision=lax.Precision.HIGHEST)
    acum_r = jnp.dot(da_r, tri_r, preferred_element_type=F32, precision=lax.Precision.HIGHEST)
    alast_c = jnp.sum(da_c, axis=0, keepdims=True)
    pw = 2 * SSD_HEAD_DIM
    first_q = lax.broadcasted_iota(jnp.int32, (q, pw), 1) < SSD_HEAD_DIM
    first_1 = lax.broadcasted_iota(jnp.int32, (1, pw), 1) < SSD_HEAD_DIM

    for g in range(SSD_GROUPS):
        bg = b_ref[0, :, g * SSD_STATE:(g + 1) * SSD_STATE].astype(BF16)
        cg = c_ref[0, :, g * SSD_STATE:(g + 1) * SSD_STATE].astype(BF16)
        cb = lax.dot_general(cg, bg, (((1,), (1,)), ((), ())), preferred_element_type=F32)
        h_prev = state_ref[g]
        y_off = jnp.dot(cg, h_prev.astype(BF16), preferred_element_type=F32)
        xw, dec = [], []
        for pr in range(SSD_HPG // 2):
            pcol = slice(g * SSD_GW + pr * pw, g * SSD_GW + (pr + 1) * pw)
            xs_p = xs_ref[0, :, pcol]
            xs_b = xs_p.astype(BF16)
            y_d, e_ac, w_st, e_al = [], [], [], []
            for r in (2 * pr, 2 * pr + 1):
                h = g * SSD_HPG + r
                ac = acum_c[:, h:h + 1]
                ar = acum_r[h:h + 1, :]
                decay = jnp.exp(jnp.where(keep, ac - ar, NEG_BIG))
                mix = (cb * decay * dtr[h:h + 1, :]).astype(BF16)
                y_d.append(jnp.dot(mix, xs_b, preferred_element_type=F32))
                al = alast_c[:, h:h + 1]
                e_ac.append(jnp.exp(ac))
                w_st.append(jnp.exp(al - ac) * dtc[:, h:h + 1])
                e_al.append(jnp.exp(al))
            y_ref[0, 0, :, pcol] = (jnp.where(first_q, y_d[0], y_d[1])
                                    + y_off[:, pr * pw:(pr + 1) * pw] * jnp.where(first_q, e_ac[0], e_ac[1]))
            xw.append((xs_p * jnp.where(first_q, w_st[0], w_st[1])).astype(BF16))
            dec.append(jnp.where(first_1, e_al[0], e_al[1]))
        s_new = lax.dot_general(bg, jnp.concatenate(xw, axis=1), (((0,), (0,)), ((), ())),
                                preferred_element_type=F32)
        state_ref[g] = h_prev * jnp.concatenate(dec, axis=1) + s_new


def ssd_scan_joint(xbc, dt, dtt, a_row, a_col, n_ctx_chunks):
    bsz, ln, _ = xbc.shape
    nc = ln // SSD_CHUNK
    q = SSD_CHUNK

    def pos(d, c):
        back = jnp.where(c < n_ctx_chunks, n_ctx_chunks - 1 - c, nc + n_ctx_chunks - 1 - c)
        return jnp.where(d == 0, c, back)

    return pl.pallas_call(
        _ssd_kernel,
        grid=(2, bsz, nc),
        in_specs=[pl.BlockSpec((1, q, SSD_WIDTH), lambda d, b, c: (b, pos(d, c), 0)),
                  pl.BlockSpec((1, q, SSD_GN), lambda d, b, c: (b, pos(d, c), SSD_WIDTH // SSD_GN)),
                  pl.BlockSpec((1, q, SSD_GN), lambda d, b, c: (b, pos(d, c), SSD_WIDTH // SSD_GN + 1)),
                  pl.BlockSpec((1, 1, q, SSD_HEADS), lambda d, b, c: (d, b, pos(d, c), 0)),
                  pl.BlockSpec((1, 1, SSD_HEADS, q), lambda d, b, c: (d, b, 0, pos(d, c))),
                  pl.BlockSpec((1, 1, SSD_HEADS), lambda d, b, c: (d, 0, 0)),
                  pl.BlockSpec((1, SSD_HEADS, 1), lambda d, b, c: (d, 0, 0))],
        out_specs=pl.BlockSpec((1, 1, q, SSD_WIDTH), lambda d, b, c: (d, b, pos(d, c), 0)),
        out_shape=jax.ShapeDtypeStruct((2, bsz, ln, SSD_WIDTH), F32),
        scratch_shapes=[pltpu.VMEM((SSD_GROUPS, SSD_STATE, SSD_GW), F32)],
        compiler_params=_cparams(("parallel", "parallel", "arbitrary")),
        name="ssd_scan",
    )(xbc, xbc, xbc, dt, dtt, a_row, a_col)


EO_ROWS = 16


def _even_out_kernel(tq, yf_ref, yb_ref, xs_ref, zs_ref, cu_ref, dv_ref, ng_ref, lg_ref, lb_ref, o_ref):
    gw = SSD_WIDTH // SSD_GROUPS

    def body(r, carry):
        rows = pl.ds(pl.multiple_of(r * EO_ROWS, EO_ROWS), EO_ROWS)
        y = yf_ref[0, 0, rows, :] + yb_ref[0, 0, rows, :]
        y = y + dv_ref[...] * xs_ref[0, rows, :]
        y = y * zs_ref[rows, :]
        for g in range(SSD_GROUPS):
            sl = slice(g * gw, (g + 1) * gw)
            yg = y[:, sl]
            ms = jnp.mean(yg * yg, axis=-1, keepdims=True)
            o_ref[rows, sl] = (yg * lax.rsqrt(ms + EPS) * ng_ref[:, sl]).astype(o_ref.dtype)
        cv = cu_ref[0, rows, :]
        mu = jnp.mean(cv, axis=-1, keepdims=True)
        xc = cv - mu
        var = jnp.mean(xc * xc, axis=-1, keepdims=True)
        ln = xc * lax.rsqrt(var + EPS) * lg_ref[...] + lb_ref[...]
        o_ref[rows, SSD_WIDTH:SSD_WIDTH + CM_WIDTH] = _silu(ln).astype(o_ref.dtype)
        return carry
    lax.fori_loop(0, tq // EO_ROWS, body, 0, unroll=2)


def even_out(y, xbc, zs, cu, dvec, ng, lg, lb, *, blk0, nblk, tq):
    bsz = xbc.shape[0]
    vec = pl.BlockSpec((1, SSD_WIDTH), lambda b, i: (0, 0))
    return pl.pallas_call(
        functools.partial(_even_out_kernel, tq),
        grid=(bsz, nblk),
        in_specs=[pl.BlockSpec((1, 1, tq, SSD_WIDTH), lambda b, i: (0, b, blk0 + i, 0)),
                  pl.BlockSpec((1, 1, tq, SSD_WIDTH), lambda b, i: (1, b, blk0 + i, 0)),
                  pl.BlockSpec((1, tq, SSD_WIDTH), lambda b, i: (b, blk0 + i, 0)),
                  pl.BlockSpec((tq, SSD_WIDTH), lambda b, i: (b * nblk + i, 0)),
                  pl.BlockSpec((1, tq, CM_WIDTH), lambda b, i: (b, blk0 + i, 0)),
                  vec, vec, vec, vec],
        out_specs=pl.BlockSpec((tq, SSD_WIDTH + CM_WIDTH), lambda b, i: (b * nblk + i, 0)),
        out_shape=jax.ShapeDtypeStruct((bsz * nblk * tq, SSD_WIDTH + CM_WIDTH), BF16),
        compiler_params=_cparams(("parallel", "parallel")),
        name="even_mixer_tail",
    )(y, y, xbc, zs, cu, dvec, ng, lg, lb)


_NT = (((1,), (1,)), ((), ()))


def _na_kernel(rows_total, q_ref, k_ref, v_ref, kc_ref, vc_ref, bias_ref, o_ref):
    rb = pl.program_id(2)
    k0 = jnp.clip(rb * NA_QROWS - NA_ROWS // 2, 0, rows_total - NA_KROWS) * GRID_W
    k0 = pl.multiple_of(k0, GRID_W)
    nk = NA_KROWS * GRID_W
    for h in range(NA_HPS):
        hc = slice(h * NA_HEAD_DIM, (h + 1) * NA_HEAD_DIM)
        qv = q_ref[:, hc]
        s = lax.dot_general(qv, k_ref[pl.ds(k0, nk), hc], _NT, preferred_element_type=F32) + bias_ref[0, h]
        sc = lax.dot_general(qv, kc_ref[:, hc], _NT, preferred_element_type=F32)
        m = jnp.maximum(jnp.max(s, axis=-1, keepdims=True), jnp.max(sc, axis=-1, keepdims=True))
        p = jnp.exp(s - m)
        pc = jnp.exp(sc - m)
        den = jnp.sum(p, axis=-1, keepdims=True) + jnp.sum(pc, axis=-1, keepdims=True)
        o = jnp.dot(p.astype(BF16), v_ref[pl.ds(k0, nk), hc], preferred_element_type=F32)
        o = o + jnp.dot(pc.astype(BF16), vc_ref[:, hc], preferred_element_type=F32)
        o_ref[:, hc] = (o / den).astype(o_ref.dtype)


def neighbourhood_attention(q, k, v, kc, vc, bias, bsz, seq, n_ctx):
    rows = seq // GRID_W
    nrb = rows // NA_QROWS
    tq = NA_QROWS * GRID_W
    nk = NA_KROWS * GRID_W
    hw = NA_HPS * NA_HEAD_DIM

    def btype(rb):
        return jnp.where(rb == 0, 0, jnp.where(rb == nrb - 1, 2, 1))

    return pl.pallas_call(
        functools.partial(_na_kernel, rows),
        grid=(NA_HEADS // NA_HPS, bsz, nrb),
        in_specs=[pl.BlockSpec((tq, hw), lambda h, b, r: (b * nrb + r, h)),
                  pl.BlockSpec((seq, hw), lambda h, b, r: (b, h)),
                  pl.BlockSpec((seq, hw), lambda h, b, r: (b, h)),
                  pl.BlockSpec((n_ctx, hw), lambda h, b, r: (b, h)),
                  pl.BlockSpec((n_ctx, hw), lambda h, b, r: (b, h)),
                  pl.BlockSpec((1, NA_HPS, tq, nk), lambda h, b, r: (btype(r), h, 0, 0))],
        out_specs=pl.BlockSpec((tq, hw), lambda h, b, r: (b * nrb + r, h)),
        out_shape=jax.ShapeDtypeStruct((bsz * seq, NA_WIDTH), BF16),
        compiler_params=_cparams(("parallel", "parallel", "arbitrary")),
        name="neighbourhood_attention",
    )(q, k, v, kc, vc, bias)


def _ctx_attn_kernel(q_ref, k_ref, v_ref, o_ref):
    s = lax.dot_general(q_ref[...], k_ref[...], _NT, preferred_element_type=F32)
    m = jnp.max(s, axis=-1, keepdims=True)
    p = jnp.exp(s - m)
    den = jnp.sum(p, axis=-1, keepdims=True)
    o = jnp.dot(p.astype(BF16), v_ref[...], preferred_element_type=F32)
    o_ref[...] = (o / den).astype(o_ref.dtype)


def context_attention(q, k, v, bsz, n_ctx):
    spec = pl.BlockSpec((n_ctx, NA_HEAD_DIM), lambda b, h: (b, h))
    return pl.pallas_call(
        _ctx_attn_kernel,
        grid=(bsz, NA_HEADS),
        in_specs=[spec, spec, spec],
        out_specs=spec,
        out_shape=jax.ShapeDtypeStruct((bsz * n_ctx, NA_WIDTH), BF16),
        compiler_params=_cparams(("parallel", "parallel")),
        name="context_attention",
    )(q, k, v)


def _na_bias_tables(rpb, rows):
    nrb = rows // NA_QROWS
    col = np.arange(GRID_W)
    c0 = np.clip(col - NA_COLS // 2, 0, GRID_W - NA_COLS)
    col_ok = (col[None, :] >= c0[:, None]) & (col[None, :] < c0[:, None] + NA_COLS)
    col_idx = np.clip(col[None, :] - col[:, None] + NA_COLS - 1, 0, 2 * NA_COLS - 2)
    kr = min(NA_ROWS, rows)
    ridx = np.zeros((3, NA_QROWS, NA_KROWS), np.int32)
    rok = np.zeros((3, NA_QROWS, NA_KROWS), bool)
    for t, rb in enumerate((0, 1, nrb - 1)):
        k0 = int(np.clip(rb * NA_QROWS - NA_ROWS // 2, 0, rows - NA_KROWS))
        for rq in range(NA_QROWS):
            r = rb * NA_QROWS + rq
            r0 = int(np.clip(r - kr // 2, 0, rows - kr))
            for rk in range(NA_KROWS):
                ra = k0 + rk
                rok[t, rq, rk] = r0 <= ra < r0 + kr
                ridx[t, rq, rk] = int(np.clip(ra - r + NA_ROWS - 1, 0, 2 * NA_ROWS - 2))
    n_off = 2 * NA_ROWS - 1
    tz = rpb[:, :, col_idx].astype(F32)
    tz = jnp.where(col_ok[None, None], tz, NEG_BIG)
    tz = jnp.concatenate([tz, jnp.full((NA_HEADS, 1, GRID_W, GRID_W), NEG_BIG, F32)], axis=1)
    tile_of = np.where(rok, ridx, n_off)

    def build(tz_ref, o_ref):
        for t in range(3):
            @pl.when(pl.program_id(0) == t)
            def _(t=t):
                for rq in range(NA_QROWS):
                    for pr in range(NA_KROWS // 2):
                        a, b = int(tile_of[t, rq, 2 * pr]), int(tile_of[t, rq, 2 * pr + 1])
                        o_ref[0, 0, rq * GRID_W:(rq + 1) * GRID_W, 2 * pr * GRID_W:(2 * pr + 2) * GRID_W] = (
                            jnp.concatenate([tz_ref[0, a], tz_ref[0, b]], axis=1))

    return pl.pallas_call(
        build,
        grid=(3, NA_HEADS),
        in_specs=[pl.BlockSpec((1, n_off + 1, GRID_W, GRID_W), lambda t, h: (h, 0, 0, 0))],
        out_specs=pl.BlockSpec((1, 1, NA_QROWS * GRID_W, NA_KROWS * GRID_W), lambda t, h: (t, h, 0, 0)),
        out_shape=jax.ShapeDtypeStruct((3, NA_HEADS, NA_QROWS * GRID_W, NA_KROWS * GRID_W), F32),
        compiler_params=_cparams(("parallel", "parallel")),
        name="na_bias_tables",
    )(tz)


class _Stream:
    def __init__(self, mod, rows_per_mod, tm):
        self.mod = mod
        self.rows_per_mod = rows_per_mod
        self.tm = tm

    def m(self, idx):
        return self.mod[:, idx][:, None, :]


def _ffn_half(hs, sts, g, w_in, w_out, j):
    xns = [adaln(h, st.m(3 * j), st.m(3 * j + 1), g, rows_per_mod=st.rows_per_mod) for h, st in zip(hs, sts)]
    hids = proj_matmul(xns, [w_in, w_in], [0, FFN_DIM], FFN_DIM, _ep_swiglu, BF16, tms=[st.tm for st in sts],
                       tn=512, name="ffn_in")
    return [matmul_residual([hid], [w_out + (0, FFN_DIM)], [0], h, st.m(3 * j + 2), 0.5,
                            rows_per_mod=st.rows_per_mod, tm=st.tm, tn=512, name="ffn_out")
            for hid, h, st in zip(hids, hs, sts)]


def _proj(xns, sts, w, names, n_out, ep, dtype, extras=(), name="mixer_in"):
    return proj_matmul(xns, [w[n][0] for n in names], [w[n][1] for n in names], n_out, ep, dtype,
                       tms=[st.tm for st in sts], tn=w[names[0]][2], extras=extras, w_transposed=w["transposed"],
                       name=name)


def _mixer_even(x, cx, sx, sc, g, p, bsz, seq, n_ctx):
    w = p["w_in"]
    sts = (sx, sc)
    xns = [adaln(h, st.m(3), st.m(4), g, rows_per_mod=st.rows_per_mod) for h, st in zip((x, cx), sts)]
    zs_x, zs_c = _proj(xns, sts, w, ("z",), SSD_WIDTH, _ep_silu, F32, name="ssd_z")
    xbc_x, xbc_c = _proj(xns, sts, w, ("xbc",), SSD_XBC, _ep_id, F32, name="ssd_xbc")
    dt_x, dt_c = _proj(xns, sts, w, ("dt",), 2 * SSD_HEADS, _ep_softplus_bias, F32, extras=(p["dt_bias"],),
                       name="ssd_dt")
    u_x, u_c = _proj(xns, sts, w, ("ga", "gg"), CM_WIDTH, _ep_glu_sigmoid, F32, name="conformer_glu")
    xbc = conv3_silu_joint(xbc_c.reshape(bsz, n_ctx, SSD_XBC), xbc_x.reshape(bsz, seq, SSD_XBC),
                           p["conv_w"], p["conv_b"])
    dt = jnp.concatenate([dt_c.reshape(bsz, n_ctx, 2, SSD_HEADS), dt_x.reshape(bsz, seq, 2, SSD_HEADS)], axis=1)
    dt = jnp.transpose(dt, (2, 0, 1, 3))
    dtt = jnp.transpose(dt, (0, 1, 3, 2))
    y = ssd_scan_joint(xbc, dt, dtt, p["a_row"], p["a_col"], n_ctx // SSD_CHUNK)
    cu = conv31_joint(u_c.reshape(bsz, n_ctx, CM_WIDTH), u_x.reshape(bsz, seq, CM_WIDTH), p["cm_w"], p["cm_b"])
    tq = n_ctx
    tail = functools.partial(even_out, y, xbc, dvec=p["dvec"], ng=p["norm_g"], lg=p["ln_g"], lb=p["ln_b"], tq=tq)
    lhs_x = tail(zs=zs_x, cu=cu, blk0=1, nblk=seq // tq)
    lhs_c = tail(zs=zs_c, cu=cu, blk0=0, nblk=1)
    return lhs_x, lhs_c


def _mixer_odd(x, cx, sx, sc, g, p, bsz, seq, n_ctx, need_ctx):
    w = p["w_in"]
    hg_q, hg_k = p["q_g"], p["k_g"]
    ep_q = functools.partial(_ep_head_rms, NA_HEAD_DIM ** -0.5)
    ep_k = functools.partial(_ep_head_rms, None)
    xn_x = adaln(x, sx.m(3), sx.m(4), g, rows_per_mod=sx.rows_per_mod)
    xn_c = adaln(cx, sc.m(3), sc.m(4), g, rows_per_mod=sc.rows_per_mod)
    both, xns = (sx, sc), (xn_x, xn_c)
    some = (both, xns) if need_ctx else ((sx,), (xn_x,))
    k, kc = _proj(xns, both, w, ("k",), NA_WIDTH, ep_k, BF16, extras=(hg_k,), name="na_k")
    v, vc = _proj(xns, both, w, ("v",), NA_WIDTH, _ep_id, BF16, name="na_v")
    q = _proj(some[1], some[0], w, ("q",), NA_WIDTH, ep_q, BF16, extras=(hg_q,), name="na_q")
    gb = _proj(some[1], some[0], w, ("gb",), SC_WIDTH, _ep_id, F32, name="sc_gate")
    pr = _proj(some[1], some[0], w, ("gc", "hs"), SC_WIDTH, _ep_mul, F32, name="sc_prod")
    ox = neighbourhood_attention(q[0], k, v, kc, vc, p["bias"], bsz, seq, n_ctx)
    yx = gated_conv3(pr[0].reshape(bsz, seq, SC_WIDTH), gb[0].reshape(bsz, seq, SC_WIDTH), p["sc_w"])
    yx = yx.reshape(bsz * seq, SC_WIDTH)
    if not need_ctx:
        return (ox, yx), None
    qc, gbc, prc = q[1], gb[1], pr[1]
    oc = context_attention(qc, kc, vc, bsz, n_ctx)
    yc = gated_conv3(prc.reshape(bsz, n_ctx, SC_WIDTH), gbc.reshape(bsz, n_ctx, SC_WIDTH), p["sc_w"])
    yc = yc.reshape(bsz * n_ctx, SC_WIDTH)
    return (ox, yx), (oc, yc)


def _segments(w, prefix, names, sizes, tiles, transposed=False):
    out, acc = {"transposed": transposed}, 0
    for n, s, tn in zip(names, sizes, tiles):
        if acc % tn == 0 and (transposed or s % LANES == 0):
            out[n] = ((w, prefix), acc, tn)
        elif transposed:
            out[n] = ((w[prefix + (slice(acc, acc + s),)], ()), 0, tn)
        else:
            out[n] = ((w[prefix][:, acc:acc + s], ()), 0, tn)
        acc += s
    return out


def kernel(x, c, ctx, c_ctx, w_mod, b_mod, norm_g, ffn_w_in, ffn_w_out, ev_w_in, ev_w_out, ssd_conv_w, ssd_conv_b,
           ssd_dt_bias, ssd_a_log, ssd_d, ssd_norm_g, cm_conv_w, cm_conv_b, cm_ln_g, cm_ln_b, od_w_in, od_w_out,
           na_q_g, na_k_g, na_rpb, sc_conv_w):
    bsz, seq, d = x.shape
    n_ctx = ctx.shape[1]
    rows = seq // GRID_W

    cvec = jnp.concatenate([c, c_ctx[None, :], jnp.zeros((8 - bsz - 1, d), F32)], axis=0)
    mod = modulation(cvec, w_mod, b_mod).reshape(DEPTH, 8, N_MOD, d)

    ffn_out_b = ffn_w_out.astype(BF16)
    ev_out_b = ev_w_out.astype(BF16)
    od_out_b = od_w_out.astype(BF16)
    ev_w_in_t = jnp.swapaxes(ev_w_in, 1, 2)

    xs = x.reshape(bsz * seq, d)
    cs = ctx.reshape(bsz * n_ctx, d)
    for i in range(DEPTH):
        last = i == DEPTH - 1
        sx = _Stream(mod[i, :bsz], seq, 1024)
        sc = _Stream(mod[i, bsz:bsz + 1], bsz * n_ctx, bsz * n_ctx)
        g = norm_g[i]
        xs, cs = _ffn_half((xs, cs), (sx, sc), g[0:1], (ffn_w_in, (i, 0)), (ffn_out_b, (i, 0)), 0)
        j = i // 2
        if i % 2 == 0:
            names = ("z", "xbc", "dt", "ga", "gg")
            sizes = (SSD_WIDTH, SSD_XBC, 2 * SSD_HEADS, CM_WIDTH, CM_WIDTH)
            a_neg = -jnp.exp(ssd_a_log[j].astype(F32))
            p = {
                "w_in": _segments(ev_w_in_t, (j,), names, sizes, (1024, 1024, 2 * SSD_HEADS, 512, 512), True),
                "dt_bias": ssd_dt_bias[j].reshape(1, 2 * SSD_HEADS).astype(F32),
                "conv_w": ssd_conv_w[j], "conv_b": ssd_conv_b[j][None, :],
                "a_row": a_neg[:, None, :], "a_col": a_neg[:, :, None],
                "cm_w": cm_conv_w[j], "cm_b": cm_conv_b[j][None, :],
                "dvec": jnp.repeat(ssd_d[j], SSD_HEAD_DIM)[None, :],
                "norm_g": ssd_norm_g[j][None, :], "ln_g": cm_ln_g[j][None, :], "ln_b": cm_ln_b[j][None, :],
            }
            lhs_x, lhs_c = _mixer_even(xs, cs, sx, sc, g[1:2], p, bsz, seq, n_ctx)
            w_o = [(ev_out_b, (j,), 0, SSD_WIDTH + CM_WIDTH)]
            xs = matmul_residual([lhs_x], w_o, [0], xs, sx.m(5), 1.0,
                                 rows_per_mod=sx.rows_per_mod, tm=sx.tm, tn=512, name="mixer_out")
            cs_mix = [lhs_c], w_o, [0]
        else:
            names = ("q", "k", "v", "gb", "gc", "hs")
            p = {
                "w_in": _segments(od_w_in, (j,), names, (NA_WIDTH,) * 3 + (SC_WIDTH,) * 3, (1024,) * 4 + (512,) * 2),
                "q_g": na_q_g[j][None, :], "k_g": na_k_g[j][None, :],
                "bias": _na_bias_tables(na_rpb[j], rows), "sc_w": sc_conv_w[j],
            }
            lx, lc = _mixer_odd(xs, cs, sx, sc, g[1:2], p, bsz, seq, n_ctx, not last)
            w_o = [(od_out_b, (j,), 0, NA_WIDTH), (od_out_b, (j,), 1, SC_WIDTH)]
            xs = matmul_residual(list(lx), w_o, [0, 0], xs, sx.m(5), 1.0,
                                 rows_per_mod=sx.rows_per_mod, tm=sx.tm, tn=512, name="mixer_out")
            cs_mix = (list(lc), w_o, [0, 0]) if lc is not None else None
        if last:
            (xs,) = _ffn_half((xs,), (sx,), g[2:3], (ffn_w_in, (i, 1)), (ffn_out_b, (i, 1)), 2)
        else:
            cs = matmul_residual(cs_mix[0], cs_mix[1], cs_mix[2], cs, sc.m(5), 1.0,
                                 rows_per_mod=sc.rows_per_mod, tm=sc.tm, tn=512, name="mixer_out")
            xs, cs = _ffn_half((xs, cs), (sx, sc), g[2:3], (ffn_w_in, (i, 1)), (ffn_out_b, (i, 1)), 2)
    return xs.reshape(bsz, seq, d)
```

```python
import functools

import numpy as np
import jax
import jax.numpy as jnp
from jax import lax
from jax.experimental import pallas as pl
from jax.experimental.pallas import tpu as pltpu

F32 = jnp.float32
BF16 = jnp.bfloat16

D_MODEL = 2048
DEPTH = 4
GRID_W = 64
N_MOD = 9
FFN_DIM = 5632
SSD_HEADS = 32
SSD_HEAD_DIM = 64
SSD_WIDTH = SSD_HEADS * SSD_HEAD_DIM
SSD_GROUPS = 8
SSD_STATE = 128
SSD_CHUNK = 128
SSD_GN = SSD_GROUPS * SSD_STATE
SSD_XBC = SSD_WIDTH + 2 * SSD_GN
SSD_HPG = SSD_HEADS // SSD_GROUPS
SSD_GW = SSD_HPG * SSD_HEAD_DIM
CM_WIDTH = 2048
CM_CONV = 31
NA_HEADS = 16
NA_HEAD_DIM = 128
NA_WIDTH = NA_HEADS * NA_HEAD_DIM
NA_ROWS = 8
NA_COLS = 16
SC_WIDTH = 2048
EPS = 1e-6

NA_QROWS = 8
NA_KROWS = 16
NA_HPS = 4
NEG_BIG = -1e30

VMEM_LIMIT = 56 * 1024 * 1024


def _cparams(sem):
    return pltpu.CompilerParams(dimension_semantics=sem, vmem_limit_bytes=VMEM_LIMIT)


def _silu(v):
    return v * jax.nn.sigmoid(v)


def _mod_kernel(c_ref, w_ref, b_ref, o_ref):
    s = _silu(c_ref[...]).astype(BF16)
    o_ref[0] = jnp.dot(s, w_ref[0].astype(BF16), preferred_element_type=F32) + b_ref[0]


def modulation(cvec, w_mod, b_mod, tn=1024):
    depth, d, n = w_mod.shape
    rows = cvec.shape[0]
    return pl.pallas_call(
        _mod_kernel,
        grid=(depth, n // tn),
        in_specs=[pl.BlockSpec((rows, d), lambda l, j: (0, 0)),
                  pl.BlockSpec((1, d, tn), lambda l, j: (l, 0, j)),
                  pl.BlockSpec((1, 1, tn), lambda l, j: (l, 0, j))],
        out_specs=pl.BlockSpec((1, rows, tn), lambda l, j: (l, 0, j)),
        out_shape=jax.ShapeDtypeStruct((depth, rows, n), F32),
        compiler_params=_cparams(("parallel", "parallel")),
        name="modulation",
    )(cvec, w_mod, b_mod.reshape(depth, 1, n))


NORM_ROWS = 16
NORM_UNROLL = 4


def _adaln_kernel(tm, x_ref, sh_ref, sc_ref, g_ref, o_ref):
    def body(r, carry):
        rows = pl.ds(pl.multiple_of(r * NORM_ROWS, NORM_ROWS), NORM_ROWS)
        xv = x_ref[rows, :]
        ms = jnp.mean(xv * xv, axis=-1, keepdims=True)
        y = xv * lax.rsqrt(ms + EPS) * g_ref[...]
        y = y * (1.0 + sc_ref[0]) + sh_ref[0]
        o_ref[rows, :] = y.astype(o_ref.dtype)
        return carry
    lax.fori_loop(0, tm // NORM_ROWS, body, 0, unroll=NORM_UNROLL)


def adaln(x, shift, scale, g, *, rows_per_mod):
    t, d = x.shape
    tm = min(1024, rows_per_mod)
    assert t % tm == 0 and rows_per_mod % tm == 0
    mod_spec = pl.BlockSpec((1, 1, d), lambda i: ((i * tm) // rows_per_mod, 0, 0))
    return pl.pallas_call(
        functools.partial(_adaln_kernel, tm),
        grid=(t // tm,),
        in_specs=[pl.BlockSpec((tm, d), lambda i: (i, 0)), mod_spec, mod_spec, pl.BlockSpec((1, d), lambda i: (0, 0))],
        out_specs=pl.BlockSpec((tm, d), lambda i: (i, 0)),
        out_shape=jax.ShapeDtypeStruct((t, d), BF16),
        compiler_params=_cparams(("parallel",)),
        name="adaln",
    )(x, shift, scale, g)


def _pm_kernel(n_w, n_extra, epilogue, w_transposed, tiles, *refs):
    n_s = len(tiles)
    xn_refs = refs[:n_s]
    w_refs = refs[n_s:n_s + n_w]
    extra = refs[n_s + n_w:n_s + n_w + n_extra]
    o_refs = refs[n_s + n_w + n_extra:2 * n_s + n_w + n_extra]
    wb_refs = refs[2 * n_s + n_w + n_extra:]
    i = pl.program_id(1)

    @pl.when(i == 0)
    def _():
        for w, wb in zip(w_refs, wb_refs):
            wv = w[...]
            wb[...] = (wv.T if w_transposed else wv).astype(BF16)

    lo = 0
    for xn_ref, o_ref, nt in zip(xn_refs, o_refs, tiles):
        def run(xn_ref=xn_ref, o_ref=o_ref):
            xn = xn_ref[...]
            accs = [jnp.dot(xn, wb[...], preferred_element_type=F32) for wb in wb_refs]
            epilogue(accs, extra, o_ref)
        if n_s == 1:
            run()
        else:
            pl.when((i >= lo) & (i < lo + nt))(run)
        lo += nt


def proj_matmul(xns, ws, col_offs, n_out, epilogue, out_dtype, *, tms, tn, extras=(), w_transposed=False,
                name="proj_matmul"):
    d = xns[0].shape[1]
    assert n_out % tn == 0
    for off in col_offs:
        assert off % tn == 0
    tiles, los, lo = [], [], 0
    for xn, tm in zip(xns, tms):
        assert xn.shape[0] % tm == 0
        tiles.append(xn.shape[0] // tm)
        los.append(lo)
        lo += tiles[-1]

    def row_tile(i, s):
        return jnp.clip(i - los[s], 0, tiles[s] - 1)

    in_specs = [pl.BlockSpec((tm, d), functools.partial(lambda j, i, s: (row_tile(i, s), 0), s=s))
                for s, tm in enumerate(tms)]
    for (_, prefix), off in zip(ws, col_offs):
        if w_transposed:
            in_specs.append(pl.BlockSpec((None,) * len(prefix) + (tn, d),
                                         functools.partial(lambda j, i, p, o: p + (j + o, 0), p=prefix, o=off // tn)))
        else:
            in_specs.append(pl.BlockSpec((None,) * len(prefix) + (d, tn),
                                         functools.partial(lambda j, i, p, o: p + (0, j + o), p=prefix, o=off // tn)))
    for e in extras:
        if e.shape[1] == n_out:
            in_specs.append(pl.BlockSpec((1, tn), lambda j, i: (0, j)))
        else:
            in_specs.append(pl.BlockSpec(e.shape, lambda j, i: (0, 0)))
    return pl.pallas_call(
        functools.partial(_pm_kernel, len(ws), len(extras), epilogue, w_transposed, tuple(tiles)),
        grid=(n_out // tn, lo),
        in_specs=in_specs,
        out_specs=[pl.BlockSpec((tm, tn), functools.partial(lambda j, i, s: (row_tile(i, s), j), s=s))
                   for s, tm in enumerate(tms)],
        out_shape=[jax.ShapeDtypeStruct((xn.shape[0], n_out), out_dtype) for xn in xns],
        scratch_shapes=[pltpu.VMEM((d, tn), BF16) for _ in ws],
        compiler_params=_cparams(("arbitrary", "arbitrary")),
        name=name,
    )(*xns, *[w for w, _ in ws], *extras)


def _ep_swiglu(accs, extra, o_ref):
    a, g = accs
    o_ref[...] = (_silu(a) * g).astype(o_ref.dtype)


def _ep_glu_sigmoid(accs, extra, o_ref):
    a, g = accs
    o_ref[...] = (a * jax.nn.sigmoid(g)).astype(o_ref.dtype)


def _ep_mul(accs, extra, o_ref):
    a, g = accs
    o_ref[...] = (a * g).astype(o_ref.dtype)


def _ep_silu(accs, extra, o_ref):
    o_ref[...] = _silu(accs[0]).astype(o_ref.dtype)


def _ep_id(accs, extra, o_ref):
    o_ref[...] = accs[0].astype(o_ref.dtype)


def _ep_softplus_bias(accs, extra, o_ref):
    v = accs[0] + extra[0][...]
    o_ref[...] = jnp.maximum(v, 0.0) + jnp.log1p(jnp.exp(-jnp.abs(v)))


def _ep_head_rms(post_scale, accs, extra, o_ref):
    a = accs[0]
    hg = extra[0][...]
    for h in range(a.shape[1] // NA_HEAD_DIM):
        sl = slice(h * NA_HEAD_DIM, (h + 1) * NA_HEAD_DIM)
        ah = a[:, sl]
        ms = jnp.mean(ah * ah, axis=-1, keepdims=True)
        y = ah * lax.rsqrt(ms + EPS) * hg
        if post_scale is not None:
            y = y * post_scale
        o_ref[:, sl] = y.astype(o_ref.dtype)


def _mr_kernel(n_p, coef, *refs):
    lhs = refs[:n_p]
    ws = refs[n_p:2 * n_p]
    x_ref, gate_ref, o_ref = refs[2 * n_p:]
    acc = jnp.dot(lhs[0][...], ws[0][...].astype(BF16), preferred_element_type=F32)
    for p in range(1, n_p):
        acc = acc + jnp.dot(lhs[p][...], ws[p][...].astype(BF16), preferred_element_type=F32)
    gate = gate_ref[0]
    if coef != 1.0:
        gate = coef * gate
    o_ref[...] = x_ref[...] + gate * acc


def matmul_residual(lhs_list, w_list, lhs_offs, x, gate, coef, *, rows_per_mod, tm, tn, name="matmul_residual"):
    t, d = x.shape
    assert t % tm == 0 and d % tn == 0 and rows_per_mod % tm == 0
    in_specs = []
    for l, (_, _, _, kp), off in zip(lhs_list, w_list, lhs_offs):
        assert off % kp == 0
        in_specs.append(pl.BlockSpec((tm, kp), functools.partial(lambda i, j, o: (i, o), o=off // kp)))
    for _, prefix, rblk, kp in w_list:
        in_specs.append(pl.BlockSpec((None,) * len(prefix) + (kp, tn),
                                     functools.partial(lambda i, j, p, r: p + (r, j), p=prefix, r=rblk)))
    w_list = [w for w, _, _, _ in w_list]
    in_specs.append(pl.BlockSpec((tm, tn), lambda i, j: (i, j)))
    in_specs.append(pl.BlockSpec((1, 1, tn), lambda i, j: ((i * tm) // rows_per_mod, 0, j)))
    return pl.pallas_call(
        functools.partial(_mr_kernel, len(lhs_list), coef),
        grid=(t // tm, d // tn),
        in_specs=in_specs,
        out_specs=pl.BlockSpec((tm, tn), lambda i, j: (i, j)),
        out_shape=jax.ShapeDtypeStruct((t, d), F32),
        compiler_params=_cparams(("parallel", "parallel")),
        name=name,
    )(*lhs_list, *w_list, x, gate)


def _shift_rows(v, s):
    n = v.shape[0]
    r = pltpu.roll(v, s % n, axis=0)
    row = lax.broadcasted_iota(jnp.int32, v.shape, 0)
    if s > 0:
        return jnp.where(row < s, 0.0, r)
    return jnp.where(row >= n + s, 0.0, r)


def _conv3(v, w_ref):
    return (w_ref[0:1, :] * _shift_rows(v, 1) + w_ref[1:2, :] * v) + w_ref[2:3, :] * _shift_rows(v, -1)


def _conv3_silu_kernel(n_c, xc_ref, xx_ref, w_ref, b_ref, o_ref):
    for src, lo, ln in ((xc_ref, 0, n_c), (xx_ref, n_c, xx_ref.shape[1])):
        y = _conv3(src[0], w_ref) + b_ref[...]
        o_ref[0, lo:lo + ln, :] = _silu(y)


def conv3_silu_joint(xc, xx, w, b, ct=256):
    bsz, lc, ch = xc.shape
    lx = xx.shape[1]
    return pl.pallas_call(
        functools.partial(_conv3_silu_kernel, lc),
        grid=(bsz, ch // ct),
        in_specs=[pl.BlockSpec((1, lc, ct), lambda b_, j: (b_, 0, j)),
                  pl.BlockSpec((1, lx, ct), lambda b_, j: (b_, 0, j)),
                  pl.BlockSpec((3, ct), lambda b_, j: (0, j)),
                  pl.BlockSpec((1, ct), lambda b_, j: (0, j))],
        out_specs=pl.BlockSpec((1, lc + lx, ct), lambda b_, j: (b_, 0, j)),
        out_shape=jax.ShapeDtypeStruct((bsz, lc + lx, ch), F32),
        compiler_params=_cparams(("parallel", "parallel")),
        name="ssd_conv3_silu",
    )(xc, xx, w, b)


def _gated_conv3_kernel(p_ref, gate_ref, w_ref, o_ref):
    o_ref[0] = (gate_ref[0] * _conv3(p_ref[0], w_ref)).astype(o_ref.dtype)


def gated_conv3(prod, gate, w, ct=256):
    bsz, ln, ch = prod.shape
    spec = pl.BlockSpec((1, ln, ct), lambda b_, j: (b_, 0, j))
    return pl.pallas_call(
        _gated_conv3_kernel,
        grid=(bsz, ch // ct),
        in_specs=[spec, spec, pl.BlockSpec((3, ct), lambda b_, j: (0, j))],
        out_specs=spec,
        out_shape=jax.ShapeDtypeStruct((bsz, ln, ch), BF16),
        compiler_params=_cparams(("parallel", "parallel")),
        name="short_gated_conv",
    )(prod, gate, w)


CONV_HALO = 16
CONV_ROWS = 128
SUBLANES = 8
LANES = 128


def _conv31_kernel(n_c, uc_ref, ux_ref, w_ref, b_ref, o_ref, sh_ref):
    ct = o_ref.shape[2]

    def run(src_ref, lo, ln):
        lp = ln + 2 * CONV_HALO
        zeros = jnp.zeros((CONV_HALO, ct), F32)
        sh_ref[0, 0:CONV_HALO, :] = zeros
        sh_ref[0, CONV_HALO:CONV_HALO + ln, :] = src_ref[0]
        sh_ref[0, CONV_HALO + ln:lp, :] = zeros
        p = sh_ref[0, 0:lp, :]
        for s in range(1, SUBLANES):
            sh_ref[s, 0:lp, :] = pltpu.roll(p, lp - s, axis=0)

        def body(ci, carry):
            r0 = pl.multiple_of(ci * CONV_ROWS, CONV_ROWS)
            acc = jnp.zeros((CONV_ROWS, ct), F32) + b_ref[...]
            for k in range(CM_CONV):
                a, s = divmod(k + CONV_HALO - (CM_CONV - 1) // 2, SUBLANES)
                start = pl.multiple_of(r0 + SUBLANES * a, SUBLANES)
                acc = acc + sh_ref[s, pl.ds(start, CONV_ROWS), :] * w_ref[k:k + 1, :]
            o_ref[0, pl.ds(pl.multiple_of(lo + r0, CONV_ROWS), CONV_ROWS), :] = acc
            return carry
        lax.fori_loop(0, ln // CONV_ROWS, body, 0)

    run(uc_ref, 0, n_c)
    run(ux_ref, n_c, ux_ref.shape[1])


def conv31_joint(uc, ux, w, b, ct=128):
    bsz, lc, ch = uc.shape
    lx = ux.shape[1]
    return pl.pallas_call(
        functools.partial(_conv31_kernel, lc),
        grid=(bsz, ch // ct),
        in_specs=[pl.BlockSpec((1, lc, ct), lambda b_, j: (b_, 0, j)),
                  pl.BlockSpec((1, lx, ct), lambda b_, j: (b_, 0, j)),
                  pl.BlockSpec((CM_CONV, ct), lambda b_, j: (0, j)),
                  pl.BlockSpec((1, ct), lambda b_, j: (0, j))],
        out_specs=pl.BlockSpec((1, lc + lx, ct), lambda b_, j: (b_, 0, j)),
        out_shape=jax.ShapeDtypeStruct((bsz, lc + lx, ch), F32),
        scratch_shapes=[pltpu.VMEM((SUBLANES, lx + 2 * CONV_HALO, ct), F32)],
        compiler_params=_cparams(("parallel", "parallel")),
        name="conformer_conv31",
    )(uc, ux, w, b)


def _ssd_kernel(xs_ref, b_ref, c_ref, dt_ref, dtt_ref, ar_ref, ac_ref, y_ref, state_ref):
    q = SSD_CHUNK
    sgn = 1 - 2 * pl.program_id(0)

    @pl.when(pl.program_id(2) == 0)
    def _():
        state_ref[...] = jnp.zeros(state_ref.shape, F32)

    dtc = dt_ref[0, 0]
    dtr = dtt_ref[0, 0]
    da_c = dtc * ar_ref[0]
    da_r = dtr * ac_ref[0]
    ri = lax.broadcasted_iota(jnp.int32, (q, q), 0)
    ci = lax.broadcasted_iota(jnp.int32, (q, q), 1)
    keep = (ci - ri) * sgn <= 0
    tri_c = jnp.where(keep, 1.0, 0.0)
    tri_r = jnp.where((ri - ci) * sgn <= 0, 1.0, 0.0)
    acum_c = jnp.dot(tri_c, da_c, preferred_element_type=F32, precision=lax.Precision.HIGHEST)
    acum_r = jnp.dot(da_r, tri_r, preferred_element_type=F32, precision=lax.Precision.HIGHEST)
    alast_c = jnp.sum(da_c, axis=0, keepdims=True)
    pw = 2 * SSD_HEAD_DIM
    first_q = lax.broadcasted_iota(jnp.int32, (q, pw), 1) < SSD_HEAD_DIM
    first_1 = lax.broadcasted_iota(jnp.int32, (1, pw), 1) < SSD_HEAD_DIM

    for g in range(SSD_GROUPS):
        bg = b_ref[0, :, g * SSD_STATE:(g + 1) * SSD_STATE].astype(BF16)
        cg = c_ref[0, :, g * SSD_STATE:(g + 1) * SSD_STATE].astype(BF16)
        cb = lax.dot_general(cg, bg, (((1,), (1,)), ((), ())), preferred_element_type=F32)
        h_prev = state_ref[g]
        y_off = jnp.dot(cg, h_prev.astype(BF16), preferred_element_type=F32)
        xw, dec = [], []
        for pr in range(SSD_HPG // 2):
            pcol = slice(g * SSD_GW + pr * pw, g * SSD_GW + (pr + 1) * pw)
            xs_p = xs_ref[0, :, pcol]
            xs_b = xs_p.astype(BF16)
            y_d, e_ac, w_st, e_al = [], [], [], []
            for r in (2 * pr, 2 * pr + 1):
                h = g * SSD_HPG + r
                ac = acum_c[:, h:h + 1]
                ar = acum_r[h:h + 1, :]
                decay = jnp.exp(jnp.where(keep, ac - ar, NEG_BIG))
                mix = (cb * decay * dtr[h:h + 1, :]).astype(BF16)
                y_d.append(jnp.dot(mix, xs_b, preferred_element_type=F32))
                al = alast_c[:, h:h + 1]
                e_ac.append(jnp.exp(ac))
                w_st.append(jnp.exp(al - ac) * dtc[:, h:h + 1])
                e_al.append(jnp.exp(al))
            y_ref[0, 0, :, pcol] = (jnp.where(first_q, y_d[0], y_d[1])
                                    + y_off[:, pr * pw:(pr + 1) * pw] * jnp.where(first_q, e_ac[0], e_ac[1]))
            xw.append((xs_p * jnp.where(first_q, w_st[0], w_st[1])).astype(BF16))
            dec.append(jnp.where(first_1, e_al[0], e_al[1]))
        s_new = lax.dot_general(bg, jnp.concatenate(xw, axis=1), (((0,), (0,)), ((), ())),
                                preferred_element_type=F32)
        state_ref[g] = h_prev * jnp.concatenate(dec, axis=1) + s_new


def ssd_scan_joint(xbc, dt, dtt, a_row, a_col, n_ctx_chunks):
    bsz, ln, _ = xbc.shape
    nc = ln // SSD_CHUNK
    q = SSD_CHUNK

    def pos(d, c):
        back = jnp.where(c < n_ctx_chunks, n_ctx_chunks - 1 - c, nc + n_ctx_chunks - 1 - c)
        return jnp.where(d == 0, c, back)

    return pl.pallas_call(
        _ssd_kernel,
        grid=(2, bsz, nc),
        in_specs=[pl.BlockSpec((1, q, SSD_WIDTH), lambda d, b, c: (b, pos(d, c), 0)),
                  pl.BlockSpec((1, q, SSD_GN), lambda d, b, c: (b, pos(d, c), SSD_WIDTH // SSD_GN)),
                  pl.BlockSpec((1, q, SSD_GN), lambda d, b, c: (b, pos(d, c), SSD_WIDTH // SSD_GN + 1)),
                  pl.BlockSpec((1, 1, q, SSD_HEADS), lambda d, b, c: (d, b, pos(d, c), 0)),
                  pl.BlockSpec((1, 1, SSD_HEADS, q), lambda d, b, c: (d, b, 0, pos(d, c))),
                  pl.BlockSpec((1, 1, SSD_HEADS), lambda d, b, c: (d, 0, 0)),
                  pl.BlockSpec((1, SSD_HEADS, 1), lambda d, b, c: (d, 0, 0))],
        out_specs=pl.BlockSpec((1, 1, q, SSD_WIDTH), lambda d, b, c: (d, b, pos(d, c), 0)),
        out_shape=jax.ShapeDtypeStruct((2, bsz, ln, SSD_WIDTH), F32),
        scratch_shapes=[pltpu.VMEM((SSD_GROUPS, SSD_STATE, SSD_GW), F32)],
        compiler_params=_cparams(("parallel", "parallel", "arbitrary")),
        name="ssd_scan",
    )(xbc, xbc, xbc, dt, dtt, a_row, a_col)


EO_ROWS = 16


def _even_out_kernel(tq, yf_ref, yb_ref, xs_ref, zs_ref, cu_ref, dv_ref, ng_ref, lg_ref, lb_ref, o_ref):
    gw = SSD_WIDTH // SSD_GROUPS

    def body(r, carry):
        rows = pl.ds(pl.multiple_of(r * EO_ROWS, EO_ROWS), EO_ROWS)
        y = yf_ref[0, 0, rows, :] + yb_ref[0, 0, rows, :]
        y = y + dv_ref[...] * xs_ref[0, rows, :]
        y = y * zs_ref[rows, :]
        for g in range(SSD_GROUPS):
            sl = slice(g * gw, (g + 1) * gw)
            yg = y[:, sl]
            ms = jnp.mean(yg * yg, axis=-1, keepdims=True)
            o_ref[rows, sl] = (yg * lax.rsqrt(ms + EPS) * ng_ref[:, sl]).astype(o_ref.dtype)
        cv = cu_ref[0, rows, :]
        mu = jnp.mean(cv, axis=-1, keepdims=True)
        xc = cv - mu
        var = jnp.mean(xc * xc, axis=-1, keepdims=True)
        ln = xc * lax.rsqrt(var + EPS) * lg_ref[...] + lb_ref[...]
        o_ref[rows, SSD_WIDTH:SSD_WIDTH + CM_WIDTH] = _silu(ln).astype(o_ref.dtype)
        return carry
    lax.fori_loop(0, tq // EO_ROWS, body, 0, unroll=2)


def even_out(y, xbc, zs, cu, dvec, ng, lg, lb, *, blk0, nblk, tq):
    bsz = xbc.shape[0]
    vec = pl.BlockSpec((1, SSD_WIDTH), lambda b, i: (0, 0))
    return pl.pallas_call(
        functools.partial(_even_out_kernel, tq),
        grid=(bsz, nblk),
        in_specs=[pl.BlockSpec((1, 1, tq, SSD_WIDTH), lambda b, i: (0, b, blk0 + i, 0)),
                  pl.BlockSpec((1, 1, tq, SSD_WIDTH), lambda b, i: (1, b, blk0 + i, 0)),
                  pl.BlockSpec((1, tq, SSD_WIDTH), lambda b, i: (b, blk0 + i, 0)),
                  pl.BlockSpec((tq, SSD_WIDTH), lambda b, i: (b * nblk + i, 0)),
                  pl.BlockSpec((1, tq, CM_WIDTH), lambda b, i: (b, blk0 + i, 0)),
                  vec, vec, vec, vec],
        out_specs=pl.BlockSpec((tq, SSD_WIDTH + CM_WIDTH), lambda b, i: (b * nblk + i, 0)),
        out_shape=jax.ShapeDtypeStruct((bsz * nblk * tq, SSD_WIDTH + CM_WIDTH), BF16),
        compiler_params=_cparams(("parallel", "parallel")),
        name="even_mixer_tail",
    )(y, y, xbc, zs, cu, dvec, ng, lg, lb)


_NT = (((1,), (1,)), ((), ()))


def _na_kernel(rows_total, q_ref, k_ref, v_ref, kc_ref, vc_ref, bias_ref, o_ref):
    rb = pl.program_id(2)
    k0 = jnp.clip(rb * NA_QROWS - NA_ROWS // 2, 0, rows_total - NA_KROWS) * GRID_W
    k0 = pl.multiple_of(k0, GRID_W)
    nk = NA_KROWS * GRID_W
    for h in range(NA_HPS):
        hc = slice(h * NA_HEAD_DIM, (h + 1) * NA_HEAD_DIM)
        qv = q_ref[:, hc]
        s = lax.dot_general(qv, k_ref[pl.ds(k0, nk), hc], _NT, preferred_element_type=F32) + bias_ref[0, h]
        sc = lax.dot_general(qv, kc_ref[:, hc], _NT, preferred_element_type=F32)
        m = jnp.maximum(jnp.max(s, axis=-1, keepdims=True), jnp.max(sc, axis=-1, keepdims=True))
        p = jnp.exp(s - m)
        pc = jnp.exp(sc - m)
        den = jnp.sum(p, axis=-1, keepdims=True) + jnp.sum(pc, axis=-1, keepdims=True)
        o = jnp.dot(p.astype(BF16), v_ref[pl.ds(k0, nk), hc], preferred_element_type=F32)
        o = o + jnp.dot(pc.astype(BF16), vc_ref[:, hc], preferred_element_type=F32)
        o_ref[:, hc] = (o / den).astype(o_ref.dtype)


def neighbourhood_attention(q, k, v, kc, vc, bias, bsz, seq, n_ctx):
    rows = seq // GRID_W
    nrb = rows // NA_QROWS
    tq = NA_QROWS * GRID_W
    nk = NA_KROWS * GRID_W
    hw = NA_HPS * NA_HEAD_DIM

    def btype(rb):
        return jnp.where(rb == 0, 0, jnp.where(rb == nrb - 1, 2, 1))

    return pl.pallas_call(
        functools.partial(_na_kernel, rows),
        grid=(NA_HEADS // NA_HPS, bsz, nrb),
        in_specs=[pl.BlockSpec((tq, hw), lambda h, b, r: (b * nrb + r, h)),
                  pl.BlockSpec((seq, hw), lambda h, b, r: (b, h)),
                  pl.BlockSpec((seq, hw), lambda h, b, r: (b, h)),
                  pl.BlockSpec((n_ctx, hw), lambda h, b, r: (b, h)),
                  pl.BlockSpec((n_ctx, hw), lambda h, b, r: (b, h)),
                  pl.BlockSpec((1, NA_HPS, tq, nk), lambda h, b, r: (btype(r), h, 0, 0))],
        out_specs=pl.BlockSpec((tq, hw), lambda h, b, r: (b * nrb + r, h)),
        out_shape=jax.ShapeDtypeStruct((bsz * seq, NA_WIDTH), BF16),
        compiler_params=_cparams(("parallel", "parallel", "arbitrary")),
        name="neighbourhood_attention",
    )(q, k, v, kc, vc, bias)


def _ctx_attn_kernel(q_ref, k_ref, v_ref, o_ref):
    s = lax.dot_general(q_ref[...], k_ref[...], _NT, preferred_element_type=F32)
    m = jnp.max(s, axis=-1, keepdims=True)
    p = jnp.exp(s - m)
    den = jnp.sum(p, axis=-1, keepdims=True)
    o = jnp.dot(p.astype(BF16), v_ref[...], preferred_element_type=F32)
    o_ref[...] = (o / den).astype(o_ref.dtype)


def context_attention(q, k, v, bsz, n_ctx):
    spec = pl.BlockSpec((n_ctx, NA_HEAD_DIM), lambda b, h: (b, h))
    return pl.pallas_call(
        _ctx_attn_kernel,
        grid=(bsz, NA_HEADS),
        in_specs=[spec, spec, spec],
        out_specs=spec,
        out_shape=jax.ShapeDtypeStruct((bsz * n_ctx, NA_WIDTH), BF16),
        compiler_params=_cparams(("parallel", "parallel")),
        name="context_attention",
    )(q, k, v)


def _na_bias_tables(rpb, rows):
    nrb = rows // NA_QROWS
    col = np.arange(GRID_W)
    c0 = np.clip(col - NA_COLS // 2, 0, GRID_W - NA_COLS)
    col_ok = (col[None, :] >= c0[:, None]) & (col[None, :] < c0[:, None] + NA_COLS)
    col_idx = np.clip(col[None, :] - col[:, None] + NA_COLS - 1, 0, 2 * NA_COLS - 2)
    kr = min(NA_ROWS, rows)
    ridx = np.zeros((3, NA_QROWS, NA_KROWS), np.int32)
    rok = np.zeros((3, NA_QROWS, NA_KROWS), bool)
    for t, rb in enumerate((0, 1, nrb - 1)):
        k0 = int(np.clip(rb * NA_QROWS - NA_ROWS // 2, 0, rows - NA_KROWS))
        for rq in range(NA_QROWS):
            r = rb * NA_QROWS + rq
            r0 = int(np.clip(r - kr // 2, 0, rows - kr))
            for rk in range(NA_KROWS):
                ra = k0 + rk
                rok[t, rq, rk] = r0 <= ra < r0 + kr
                ridx[t, rq, rk] = int(np.clip(ra - r + NA_ROWS - 1, 0, 2 * NA_ROWS - 2))
    n_off = 2 * NA_ROWS - 1
    tz = rpb[:, :, col_idx].astype(F32)
    tz = jnp.where(col_ok[None, None], tz, NEG_BIG)
    tz = jnp.concatenate([tz, jnp.full((NA_HEADS, 1, GRID_W, GRID_W), NEG_BIG, F32)], axis=1)
    tile_of = np.where(rok, ridx, n_off)

    def build(tz_ref, o_ref):
        for t in range(3):
            @pl.when(pl.program_id(0) == t)
            def _(t=t):
                for rq in range(NA_QROWS):
                    for pr in range(NA_KROWS // 2):
                        a, b = int(tile_of[t, rq, 2 * pr]), int(tile_of[t, rq, 2 * pr + 1])
                        o_ref[0, 0, rq * GRID_W:(rq + 1) * GRID_W, 2 * pr * GRID_W:(2 * pr + 2) * GRID_W] = (
                            jnp.concatenate([tz_ref[0, a], tz_ref[0, b]], axis=1))

    return pl.pallas_call(
        build,
        grid=(3, NA_HEADS),
        in_specs=[pl.BlockSpec((1, n_off + 1, GRID_W, GRID_W), lambda t, h: (h, 0, 0, 0))],
        out_specs=pl.BlockSpec((1, 1, NA_QROWS * GRID_W, NA_KROWS * GRID_W), lambda t, h: (t, h, 0, 0)),
        out_shape=jax.ShapeDtypeStruct((3, NA_HEADS, NA_QROWS * GRID_W, NA_KROWS * GRID_W), F32),
        compiler_params=_cparams(("parallel", "parallel")),
        name="na_bias_tables",
    )(tz)


class _Stream:
    def __init__(self, mod, rows_per_mod, tm):
        self.mod = mod
        self.rows_per_mod = rows_per_mod
        self.tm = tm

    def m(self, idx):
        return self.mod[:, idx][:, None, :]


def _ffn_half(hs, sts, g, w_in, w_out, j):
    xns = [adaln(h, st.m(3 * j), st.m(3 * j + 1), g, rows_per_mod=st.rows_per_mod) for h, st in zip(hs, sts)]
    hids = [proj_matmul([xn], [w_in, w_in], [0, FFN_DIM], FFN_DIM, _ep_swiglu, BF16, tms=[st.tm], tn=512,
                        name="ffn_in")[0] for xn, st in zip(xns, sts)]
    return [matmul_residual([hid], [w_out + (0, FFN_DIM)], [0], h, st.m(3 * j + 2), 0.5,
                            rows_per_mod=st.rows_per_mod, tm=st.tm, tn=512, name="ffn_out")
            for hid, h, st in zip(hids, hs, sts)]


def _proj(xns, sts, w, names, n_out, ep, dtype, extras=(), name="mixer_in"):
    return [proj_matmul([xn], [w[n][0] for n in names], [w[n][1] for n in names], n_out, ep, dtype,
                        tms=[st.tm], tn=w[names[0]][2], extras=extras, w_transposed=w["transposed"], name=name)[0]
            for xn, st in zip(xns, sts)]


def _mixer_even(x, cx, sx, sc, g, p, bsz, seq, n_ctx):
    w = p["w_in"]
    sts = (sx, sc)
    xns = [adaln(h, st.m(3), st.m(4), g, rows_per_mod=st.rows_per_mod) for h, st in zip((x, cx), sts)]
    zs_x, zs_c = _proj(xns, sts, w, ("z",), SSD_WIDTH, _ep_silu, F32, name="ssd_z")
    xbc_x, xbc_c = _proj(xns, sts, w, ("xbc",), SSD_XBC, _ep_id, F32, name="ssd_xbc")
    dt_x, dt_c = _proj(xns, sts, w, ("dt",), 2 * SSD_HEADS, _ep_softplus_bias, F32, extras=(p["dt_bias"],),
                       name="ssd_dt")
    u_x, u_c = _proj(xns, sts, w, ("ga", "gg"), CM_WIDTH, _ep_glu_sigmoid, F32, name="conformer_glu")
    xbc = conv3_silu_joint(xbc_c.reshape(bsz, n_ctx, SSD_XBC), xbc_x.reshape(bsz, seq, SSD_XBC),
                           p["conv_w"], p["conv_b"])
    dt = jnp.concatenate([dt_c.reshape(bsz, n_ctx, 2, SSD_HEADS), dt_x.reshape(bsz, seq, 2, SSD_HEADS)], axis=1)
    dt = jnp.transpose(dt, (2, 0, 1, 3))
    dtt = jnp.transpose(dt, (0, 1, 3, 2))
    y = ssd_scan_joint(xbc, dt, dtt, p["a_row"], p["a_col"], n_ctx // SSD_CHUNK)
    cu = conv31_joint(u_c.reshape(bsz, n_ctx, CM_WIDTH), u_x.reshape(bsz, seq, CM_WIDTH), p["cm_w"], p["cm_b"])
    tq = n_ctx
    tail = functools.partial(even_out, y, xbc, dvec=p["dvec"], ng=p["norm_g"], lg=p["ln_g"], lb=p["ln_b"], tq=tq)
    lhs_x = tail(zs=zs_x, cu=cu, blk0=1, nblk=seq // tq)
    lhs_c = tail(zs=zs_c, cu=cu, blk0=0, nblk=1)
    return lhs_x, lhs_c


def _mixer_odd(x, cx, sx, sc, g, p, bsz, seq, n_ctx, need_ctx):
    w = p["w_in"]
    hg_q, hg_k = p["q_g"], p["k_g"]
    ep_q = functools.partial(_ep_head_rms, NA_HEAD_DIM ** -0.5)
    ep_k = functools.partial(_ep_head_rms, None)
    xn_x = adaln(x, sx.m(3), sx.m(4), g, rows_per_mod=sx.rows_per_mod)
    xn_c = adaln(cx, sc.m(3), sc.m(4), g, rows_per_mod=sc.rows_per_mod)
    both, xns = (sx, sc), (xn_x, xn_c)
    some = (both, xns) if need_ctx else ((sx,), (xn_x,))
    k, kc = _proj(xns, both, w, ("k",), NA_WIDTH, ep_k, BF16, extras=(hg_k,), name="na_k")
    v, vc = _proj(xns, both, w, ("v",), NA_WIDTH, _ep_id, BF16, name="na_v")
    q = _proj(some[1], some[0], w, ("q",), NA_WIDTH, ep_q, BF16, extras=(hg_q,), name="na_q")
    gb = _proj(some[1], some[0], w, ("gb",), SC_WIDTH, _ep_id, F32, name="sc_gate")
    pr = _proj(some[1], some[0], w, ("gc", "hs"), SC_WIDTH, _ep_mul, F32, name="sc_prod")
    ox = neighbourhood_attention(q[0], k, v, kc, vc, p["bias"], bsz, seq, n_ctx)
    yx = gated_conv3(pr[0].reshape(bsz, seq, SC_WIDTH), gb[0].reshape(bsz, seq, SC_WIDTH), p["sc_w"])
    yx = yx.reshape(bsz * seq, SC_WIDTH)
    if not need_ctx:
        return (ox, yx), None
    qc, gbc, prc = q[1], gb[1], pr[1]
    oc = context_attention(qc, kc, vc, bsz, n_ctx)
    yc = gated_conv3(prc.reshape(bsz, n_ctx, SC_WIDTH), gbc.reshape(bsz, n_ctx, SC_WIDTH), p["sc_w"])
    yc = yc.reshape(bsz * n_ctx, SC_WIDTH)
    return (ox, yx), (oc, yc)


def _segments(w, prefix, names, sizes, tiles, transposed=False):
    out, acc = {"transposed": transposed}, 0
    for n, s, tn in zip(names, sizes, tiles):
        if acc % tn == 0 and (transposed or s % LANES == 0):
            out[n] = ((w, prefix), acc, tn)
        elif transposed:
            out[n] = ((w[prefix + (slice(acc, acc + s),)], ()), 0, tn)
        else:
            out[n] = ((w[prefix][:, acc:acc + s], ()), 0, tn)
        acc += s
    return out


def kernel(x, c, ctx, c_ctx, w_mod, b_mod, norm_g, ffn_w_in, ffn_w_out, ev_w_in, ev_w_out, ssd_conv_w, ssd_conv_b,
           ssd_dt_bias, ssd_a_log, ssd_d, ssd_norm_g, cm_conv_w, cm_conv_b, cm_ln_g, cm_ln_b, od_w_in, od_w_out,
           na_q_g, na_k_g, na_rpb, sc_conv_w):
    bsz, seq, d = x.shape
    n_ctx = ctx.shape[1]
    rows = seq // GRID_W

    cvec = jnp.concatenate([c, c_ctx[None, :], jnp.zeros((8 - bsz - 1, d), F32)], axis=0)
    mod = modulation(cvec, w_mod, b_mod).reshape(DEPTH, 8, N_MOD, d)

    ffn_out_b = ffn_w_out.astype(BF16)
    ev_out_b = ev_w_out.astype(BF16)
    od_out_b = od_w_out.astype(BF16)
    ev_w_in_t = jnp.swapaxes(ev_w_in, 1, 2)

    xs = x.reshape(bsz * seq, d)
    cs = ctx.reshape(bsz * n_ctx, d)
    for i in range(DEPTH):
        last = i == DEPTH - 1
        sx = _Stream(mod[i, :bsz], seq, 1024)
        sc = _Stream(mod[i, bsz:bsz + 1], bsz * n_ctx, bsz * n_ctx)
        g = norm_g[i]
        xs, cs = _ffn_half((xs, cs), (sx, sc), g[0:1], (ffn_w_in, (i, 0)), (ffn_out_b, (i, 0)), 0)
        j = i // 2
        if i % 2 == 0:
            names = ("z", "xbc", "dt", "ga", "gg")
            sizes = (SSD_WIDTH, SSD_XBC, 2 * SSD_HEADS, CM_WIDTH, CM_WIDTH)
            a_neg = -jnp.exp(ssd_a_log[j].astype(F32))
            p = {
                "w_in": _segments(ev_w_in_t, (j,), names, sizes, (1024, 1024, 2 * SSD_HEADS, 512, 512), True),
                "dt_bias": ssd_dt_bias[j].reshape(1, 2 * SSD_HEADS).astype(F32),
                "conv_w": ssd_conv_w[j], "conv_b": ssd_conv_b[j][None, :],
                "a_row": a_neg[:, None, :], "a_col": a_neg[:, :, None],
                "cm_w": cm_conv_w[j], "cm_b": cm_conv_b[j][None, :],
                "dvec": jnp.repeat(ssd_d[j], SSD_HEAD_DIM)[None, :],
                "norm_g": ssd_norm_g[j][None, :], "ln_g": cm_ln_g[j][None, :], "ln_b": cm_ln_b[j][None, :],
            }
            lhs_x, lhs_c = _mixer_even(xs, cs, sx, sc, g[1:2], p, bsz, seq, n_ctx)
            w_o = [(ev_out_b, (j,), 0, SSD_WIDTH + CM_WIDTH)]
            xs = matmul_residual([lhs_x], w_o, [0], xs, sx.m(5), 1.0,
                                 rows_per_mod=sx.rows_per_mod, tm=sx.tm, tn=512, name="mixer_out")
            cs_mix = [lhs_c], w_o, [0]
        else:
            names = ("q", "k", "v", "gb", "gc", "hs")
            p = {
                "w_in": _segments(od_w_in, (j,), names, (NA_WIDTH,) * 3 + (SC_WIDTH,) * 3, (1024,) * 4 + (512,) * 2),
                "q_g": na_q_g[j][None, :], "k_g": na_k_g[j][None, :],
                "bias": _na_bias_tables(na_rpb[j], rows), "sc_w": sc_conv_w[j],
            }
            lx, lc = _mixer_odd(xs, cs, sx, sc, g[1:2], p, bsz, seq, n_ctx, not last)
            w_o = [(od_out_b, (j,), 0, NA_WIDTH), (od_out_b, (j,), 1, SC_WIDTH)]
            xs = matmul_residual(list(lx), w_o, [0, 0], xs, sx.m(5), 1.0,
                                 rows_per_mod=sx.rows_per_mod, tm=sx.tm, tn=512, name="mixer_out")
            cs_mix = (list(lc), w_o, [0, 0]) if lc is not None else None
        if last:
            (xs,) = _ffn_half((xs,), (sx,), g[2:3], (ffn_w_in, (i, 1)), (ffn_out_b, (i, 1)), 2)
        else:
            cs = matmul_residual(cs_mix[0], cs_mix[1], cs_mix[2], cs, sc.m(5), 1.0,
                                 rows_per_mod=sc.rows_per_mod, tm=sc.tm, tn=512, name="mixer_out")
            xs, cs = _ffn_half((xs, cs), (sx, sc), g[2:3], (ffn_w_in, (i, 1)), (ffn_out_b, (i, 1)), 2)
    return xs.reshape(bsz, seq, d)
```

```python
import functools

import numpy as np
import jax
import jax.numpy as jnp
from jax import lax
from jax.experimental import pallas as pl
from jax.experimental.pallas import tpu as pltpu

F32 = jnp.float32
BF16 = jnp.bfloat16

D_MODEL = 2048
DEPTH = 4
GRID_W = 64
N_MOD = 9
FFN_DIM = 5632
SSD_HEADS = 32
SSD_HEAD_DIM = 64
SSD_WIDTH = SSD_HEADS * SSD_HEAD_DIM
SSD_GROUPS = 8
SSD_STATE = 128
SSD_CHUNK = 128
SSD_GN = SSD_GROUPS * SSD_STATE
SSD_XBC = SSD_WIDTH + 2 * SSD_GN
SSD_HPG = SSD_HEADS // SSD_GROUPS
SSD_GW = SSD_HPG * SSD_HEAD_DIM
CM_WIDTH = 2048
CM_CONV = 31
NA_HEADS = 16
NA_HEAD_DIM = 128
NA_WIDTH = NA_HEADS * NA_HEAD_DIM
NA_ROWS = 8
NA_COLS = 16
SC_WIDTH = 2048
EPS = 1e-6

NA_QROWS = 8
NA_KROWS = 16
NA_HPS = 4
NEG_BIG = -1e30

VMEM_LIMIT = 56 * 1024 * 1024


def _cparams(sem):
    return pltpu.CompilerParams(dimension_semantics=sem, vmem_limit_bytes=VMEM_LIMIT)


def _silu(v):
    return v * jax.nn.sigmoid(v)


def _mod_kernel(c_ref, w_ref, b_ref, o_ref):
    s = _silu(c_ref[...]).astype(BF16)
    o_ref[0] = jnp.dot(s, w_ref[0].astype(BF16), preferred_element_type=F32) + b_ref[0]


def modulation(cvec, w_mod, b_mod, tn=1024):
    depth, d, n = w_mod.shape
    rows = cvec.shape[0]
    return pl.pallas_call(
        _mod_kernel,
        grid=(depth, n // tn),
        in_specs=[pl.BlockSpec((rows, d), lambda l, j: (0, 0)),
                  pl.BlockSpec((1, d, tn), lambda l, j: (l, 0, j)),
                  pl.BlockSpec((1, 1, tn), lambda l, j: (l, 0, j))],
        out_specs=pl.BlockSpec((1, rows, tn), lambda l, j: (l, 0, j)),
        out_shape=jax.ShapeDtypeStruct((depth, rows, n), F32),
        compiler_params=_cparams(("parallel", "parallel")),
        name="modulation",
    )(cvec, w_mod, b_mod.reshape(depth, 1, n))


NORM_ROWS = 16
PROJ_ROW_CHUNK = 256
NORM_UNROLL = 4


def _adaln_kernel(tm, x_ref, sh_ref, sc_ref, g_ref, o_ref):
    def body(r, carry):
        rows = pl.ds(pl.multiple_of(r * NORM_ROWS, NORM_ROWS), NORM_ROWS)
        xv = x_ref[rows, :]
        ms = jnp.mean(xv * xv, axis=-1, keepdims=True)
        y = xv * lax.rsqrt(ms + EPS) * g_ref[...]
        y = y * (1.0 + sc_ref[0]) + sh_ref[0]
        o_ref[rows, :] = y.astype(o_ref.dtype)
        return carry
    lax.fori_loop(0, tm // NORM_ROWS, body, 0, unroll=NORM_UNROLL)


def adaln(x, shift, scale, g, *, rows_per_mod):
    t, d = x.shape
    tm = min(1024, rows_per_mod)
    assert t % tm == 0 and rows_per_mod % tm == 0
    mod_spec = pl.BlockSpec((1, 1, d), lambda i: ((i * tm) // rows_per_mod, 0, 0))
    return pl.pallas_call(
        functools.partial(_adaln_kernel, tm),
        grid=(t // tm,),
        in_specs=[pl.BlockSpec((tm, d), lambda i: (i, 0)), mod_spec, mod_spec, pl.BlockSpec((1, d), lambda i: (0, 0))],
        out_specs=pl.BlockSpec((tm, d), lambda i: (i, 0)),
        out_shape=jax.ShapeDtypeStruct((t, d), BF16),
        compiler_params=_cparams(("parallel",)),
        name="adaln",
    )(x, shift, scale, g)


def _pm_kernel(n_w, n_extra, epilogue, w_transposed, tiles, *refs):
    n_s = len(tiles)
    xn_refs = refs[:n_s]
    w_refs = refs[n_s:n_s + n_w]
    extra = refs[n_s + n_w:n_s + n_w + n_extra]
    o_refs = refs[n_s + n_w + n_extra:2 * n_s + n_w + n_extra]
    wb_refs = refs[2 * n_s + n_w + n_extra:]
    i = pl.program_id(1)

    @pl.when(i == 0)
    def _():
        for w, wb in zip(w_refs, wb_refs):
            wv = w[...]
            wb[...] = (wv.T if w_transposed else wv).astype(BF16)

    lo = 0
    for xn_ref, o_ref, nt in zip(xn_refs, o_refs, tiles):
        def run(xn_ref=xn_ref, o_ref=o_ref):
            rc = min(PROJ_ROW_CHUNK, xn_ref.shape[0])
            for c in range(xn_ref.shape[0] // rc):
                rows = pl.ds(c * rc, rc)
                xn = xn_ref[rows, :]
                accs = [jnp.dot(xn, wb[...], preferred_element_type=F32) for wb in wb_refs]
                epilogue(accs, extra, o_ref.at[rows, :])
        if n_s == 1:
            run()
        else:
            pl.when((i >= lo) & (i < lo + nt))(run)
        lo += nt


def proj_matmul(xns, ws, col_offs, n_out, epilogue, out_dtype, *, tms, tn, extras=(), w_transposed=False,
                name="proj_matmul"):
    d = xns[0].shape[1]
    assert n_out % tn == 0
    for off in col_offs:
        assert off % tn == 0
    tiles, los, lo = [], [], 0
    for xn, tm in zip(xns, tms):
        assert xn.shape[0] % tm == 0
        tiles.append(xn.shape[0] // tm)
        los.append(lo)
        lo += tiles[-1]

    def row_tile(i, s):
        return jnp.clip(i - los[s], 0, tiles[s] - 1)

    in_specs = [pl.BlockSpec((tm, d), functools.partial(lambda j, i, s: (row_tile(i, s), 0), s=s))
                for s, tm in enumerate(tms)]
    for (_, prefix), off in zip(ws, col_offs):
        if w_transposed:
            in_specs.append(pl.BlockSpec((None,) * len(prefix) + (tn, d),
                                         functools.partial(lambda j, i, p, o: p + (j + o, 0), p=prefix, o=off // tn)))
        else:
            in_specs.append(pl.BlockSpec((None,) * len(prefix) + (d, tn),
                                         functools.partial(lambda j, i, p, o: p + (0, j + o), p=prefix, o=off // tn)))
    for e in extras:
        if e.shape[1] == n_out:
            in_specs.append(pl.BlockSpec((1, tn), lambda j, i: (0, j)))
        else:
            in_specs.append(pl.BlockSpec(e.shape, lambda j, i: (0, 0)))
    return pl.pallas_call(
        functools.partial(_pm_kernel, len(ws), len(extras), epilogue, w_transposed, tuple(tiles)),
        grid=(n_out // tn, lo),
        in_specs=in_specs,
        out_specs=[pl.BlockSpec((tm, tn), functools.partial(lambda j, i, s: (row_tile(i, s), j), s=s))
                   for s, tm in enumerate(tms)],
        out_shape=[jax.ShapeDtypeStruct((xn.shape[0], n_out), out_dtype) for xn in xns],
        scratch_shapes=[pltpu.VMEM((d, tn), BF16) for _ in ws],
        compiler_params=_cparams(("arbitrary", "arbitrary")),
        name=name,
    )(*xns, *[w for w, _ in ws], *extras)


def _ep_swiglu(accs, extra, o_ref):
    a, g = accs
    o_ref[...] = (_silu(a) * g).astype(o_ref.dtype)


def _ep_glu_sigmoid(accs, extra, o_ref):
    a, g = accs
    o_ref[...] = (a * jax.nn.sigmoid(g)).astype(o_ref.dtype)


def _ep_mul(accs, extra, o_ref):
    a, g = accs
    o_ref[...] = (a * g).astype(o_ref.dtype)


def _ep_silu(accs, extra, o_ref):
    o_ref[...] = _silu(accs[0]).astype(o_ref.dtype)


def _ep_id(accs, extra, o_ref):
    o_ref[...] = accs[0].astype(o_ref.dtype)


def _ep_softplus_bias(accs, extra, o_ref):
    v = accs[0] + extra[0][...]
    o_ref[...] = jnp.maximum(v, 0.0) + jnp.log1p(jnp.exp(-jnp.abs(v)))


def _ep_head_rms(post_scale, accs, extra, o_ref):
    a = accs[0]
    hg = extra[0][...]
    for h in range(a.shape[1] // NA_HEAD_DIM):
        sl = slice(h * NA_HEAD_DIM, (h + 1) * NA_HEAD_DIM)
        ah = a[:, sl]
        ms = jnp.mean(ah * ah, axis=-1, keepdims=True)
        y = ah * lax.rsqrt(ms + EPS) * hg
        if post_scale is not None:
            y = y * post_scale
        o_ref[:, sl] = y.astype(o_ref.dtype)


def _mr_kernel(n_p, coef, *refs):
    lhs = refs[:n_p]
    ws = refs[n_p:2 * n_p]
    x_ref, gate_ref, o_ref = refs[2 * n_p:]
    acc = jnp.dot(lhs[0][...], ws[0][...].astype(BF16), preferred_element_type=F32)
    for p in range(1, n_p):
        acc = acc + jnp.dot(lhs[p][...], ws[p][...].astype(BF16), preferred_element_type=F32)
    gate = gate_ref[0]
    if coef != 1.0:
        gate = coef * gate
    o_ref[...] = x_ref[...] + gate * acc


def matmul_residual(lhs_list, w_list, lhs_offs, x, gate, coef, *, rows_per_mod, tm, tn, name="matmul_residual"):
    t, d = x.shape
    assert t % tm == 0 and d % tn == 0 and rows_per_mod % tm == 0
    in_specs = []
    for l, (_, _, _, kp), off in zip(lhs_list, w_list, lhs_offs):
        assert off % kp == 0
        in_specs.append(pl.BlockSpec((tm, kp), functools.partial(lambda i, j, o: (i, o), o=off // kp)))
    for _, prefix, rblk, kp in w_list:
        in_specs.append(pl.BlockSpec((None,) * len(prefix) + (kp, tn),
                                     functools.partial(lambda i, j, p, r: p + (r, j), p=prefix, r=rblk)))
    w_list = [w for w, _, _, _ in w_list]
    in_specs.append(pl.BlockSpec((tm, tn), lambda i, j: (i, j)))
    in_specs.append(pl.BlockSpec((1, 1, tn), lambda i, j: ((i * tm) // rows_per_mod, 0, j)))
    return pl.pallas_call(
        functools.partial(_mr_kernel, len(lhs_list), coef),
        grid=(t // tm, d // tn),
        in_specs=in_specs,
        out_specs=pl.BlockSpec((tm, tn), lambda i, j: (i, j)),
        out_shape=jax.ShapeDtypeStruct((t, d), F32),
        compiler_params=_cparams(("parallel", "parallel")),
        name=name,
    )(*lhs_list, *w_list, x, gate)


def _shift_rows(v, s):
    n = v.shape[0]
    r = pltpu.roll(v, s % n, axis=0)
    row = lax.broadcasted_iota(jnp.int32, v.shape, 0)
    if s > 0:
        return jnp.where(row < s, 0.0, r)
    return jnp.where(row >= n + s, 0.0, r)


def _conv3(v, w_ref):
    return (w_ref[0:1, :] * _shift_rows(v, 1) + w_ref[1:2, :] * v) + w_ref[2:3, :] * _shift_rows(v, -1)


def _conv3_silu_kernel(n_c, xc_ref, xx_ref, w_ref, b_ref, o_ref):
    for src, lo, ln in ((xc_ref, 0, n_c), (xx_ref, n_c, xx_ref.shape[1])):
        y = _conv3(src[0], w_ref) + b_ref[...]
        o_ref[0, lo:lo + ln, :] = _silu(y)


def conv3_silu_joint(xc, xx, w, b, ct=256):
    bsz, lc, ch = xc.shape
    lx = xx.shape[1]
    return pl.pallas_call(
        functools.partial(_conv3_silu_kernel, lc),
        grid=(bsz, ch // ct),
        in_specs=[pl.BlockSpec((1, lc, ct), lambda b_, j: (b_, 0, j)),
                  pl.BlockSpec((1, lx, ct), lambda b_, j: (b_, 0, j)),
                  pl.BlockSpec((3, ct), lambda b_, j: (0, j)),
                  pl.BlockSpec((1, ct), lambda b_, j: (0, j))],
        out_specs=pl.BlockSpec((1, lc + lx, ct), lambda b_, j: (b_, 0, j)),
        out_shape=jax.ShapeDtypeStruct((bsz, lc + lx, ch), F32),
        compiler_params=_cparams(("parallel", "parallel")),
        name="ssd_conv3_silu",
    )(xc, xx, w, b)


def _gated_conv3_kernel(p_ref, gate_ref, w_ref, o_ref):
    o_ref[0] = (gate_ref[0] * _conv3(p_ref[0], w_ref)).astype(o_ref.dtype)


def gated_conv3(prod, gate, w, ct=256):
    bsz, ln, ch = prod.shape
    spec = pl.BlockSpec((1, ln, ct), lambda b_, j: (b_, 0, j))
    return pl.pallas_call(
        _gated_conv3_kernel,
        grid=(bsz, ch // ct),
        in_specs=[spec, spec, pl.BlockSpec((3, ct), lambda b_, j: (0, j))],
        out_specs=spec,
        out_shape=jax.ShapeDtypeStruct((bsz, ln, ch), BF16),
        compiler_params=_cparams(("parallel", "parallel")),
        name="short_gated_conv",
    )(prod, gate, w)


CONV_HALO = 16
CONV_ROWS = 128
SUBLANES = 8
LANES = 128


def _conv31_kernel(n_c, uc_ref, ux_ref, w_ref, b_ref, o_ref, sh_ref):
    ct = o_ref.shape[2]

    def run(src_ref, lo, ln):
        lp = ln + 2 * CONV_HALO
        zeros = jnp.zeros((CONV_HALO, ct), F32)
        sh_ref[0, 0:CONV_HALO, :] = zeros
        sh_ref[0, CONV_HALO:CONV_HALO + ln, :] = src_ref[0]
        sh_ref[0, CONV_HALO + ln:lp, :] = zeros
        p = sh_ref[0, 0:lp, :]
        for s in range(1, SUBLANES):
            sh_ref[s, 0:lp, :] = pltpu.roll(p, lp - s, axis=0)

        def body(ci, carry):
            r0 = pl.multiple_of(ci * CONV_ROWS, CONV_ROWS)
            acc = jnp.zeros((CONV_ROWS, ct), F32) + b_ref[...]
            for k in range(CM_CONV):
                a, s = divmod(k + CONV_HALO - (CM_CONV - 1) // 2, SUBLANES)
                start = pl.multiple_of(r0 + SUBLANES * a, SUBLANES)
                acc = acc + sh_ref[s, pl.ds(start, CONV_ROWS), :] * w_ref[k:k + 1, :]
            o_ref[0, pl.ds(pl.multiple_of(lo + r0, CONV_ROWS), CONV_ROWS), :] = acc
            return carry
        lax.fori_loop(0, ln // CONV_ROWS, body, 0)

    run(uc_ref, 0, n_c)
    run(ux_ref, n_c, ux_ref.shape[1])


def conv31_joint(uc, ux, w, b, ct=128):
    bsz, lc, ch = uc.shape
    lx = ux.shape[1]
    return pl.pallas_call(
        functools.partial(_conv31_kernel, lc),
        grid=(bsz, ch // ct),
        in_specs=[pl.BlockSpec((1, lc, ct), lambda b_, j: (b_, 0, j)),
                  pl.BlockSpec((1, lx, ct), lambda b_, j: (b_, 0, j)),
                  pl.BlockSpec((CM_CONV, ct), lambda b_, j: (0, j)),
                  pl.BlockSpec((1, ct), lambda b_, j: (0, j))],
        out_specs=pl.BlockSpec((1, lc + lx, ct), lambda b_, j: (b_, 0, j)),
        out_shape=jax.ShapeDtypeStruct((bsz, lc + lx, ch), F32),
        scratch_shapes=[pltpu.VMEM((SUBLANES, lx + 2 * CONV_HALO, ct), F32)],
        compiler_params=_cparams(("parallel", "parallel")),
        name="conformer_conv31",
    )(uc, ux, w, b)


def _ssd_kernel(xs_ref, b_ref, c_ref, dt_ref, dtt_ref, ar_ref, ac_ref, y_ref, state_ref):
    q = SSD_CHUNK
    sgn = 1 - 2 * pl.program_id(0)

    @pl.when(pl.program_id(2) == 0)
    def _():
        state_ref[...] = jnp.zeros(state_ref.shape, F32)

    dtc = dt_ref[0, 0]
    dtr = dtt_ref[0, 0]
    da_c = dtc * ar_ref[0]
    da_r = dtr * ac_ref[0]
    ri = lax.broadcasted_iota(jnp.int32, (q, q), 0)
    ci = lax.broadcasted_iota(jnp.int32, (q, q), 1)
    keep = (ci - ri) * sgn <= 0
    tri_c = jnp.where(keep, 1.0, 0.0)
    tri_r = jnp.where((ri - ci) * sgn <= 0, 1.0, 0.0)
    acum_c = jnp.dot(tri_c, da_c, preferred_element_type=F32, precision=lax.Precision.HIGHEST)
    acum_r = jnp.dot(da_r, tri_r, preferred_element_type=F32, precision=lax.Precision.HIGHEST)
    alast_c = jnp.sum(da_c, axis=0, keepdims=True)
    pw = 2 * SSD_HEAD_DIM
    first_q = lax.broadcasted_iota(jnp.int32, (q, pw), 1) < SSD_HEAD_DIM
    first_1 = lax.broadcasted_iota(jnp.int32, (1, pw), 1) < SSD_HEAD_DIM

    for g in range(SSD_GROUPS):
        bg = b_ref[0, :, g * SSD_STATE:(g + 1) * SSD_STATE].astype(BF16)
        cg = c_ref[0, :, g * SSD_STATE:(g + 1) * SSD_STATE].astype(BF16)
        cb = lax.dot_general(cg, bg, (((1,), (1,)), ((), ())), preferred_element_type=F32)
        h_prev = state_ref[g]
        y_off = jnp.dot(cg, h_prev.astype(BF16), preferred_element_type=F32)
        xw, dec = [], []
        for pr in range(SSD_HPG // 2):
            pcol = slice(g * SSD_GW + pr * pw, g * SSD_GW + (pr + 1) * pw)
            xs_p = xs_ref[0, :, pcol]
            xs_b = xs_p.astype(BF16)
            y_d, e_ac, w_st, e_al = [], [], [], []
            for r in (2 * pr, 2 * pr + 1):
                h = g * SSD_HPG + r
                ac = acum_c[:, h:h + 1]
                ar = acum_r[h:h + 1, :]
                decay = jnp.exp(jnp.where(keep, ac - ar, NEG_BIG))
                mix = (cb * decay * dtr[h:h + 1, :]).astype(BF16)
                y_d.append(jnp.dot(mix, xs_b, preferred_element_type=F32))
                al = alast_c[:, h:h + 1]
                e_ac.append(jnp.exp(ac))
                w_st.append(jnp.exp(al - ac) * dtc[:, h:h + 1])
                e_al.append(jnp.exp(al))
            y_ref[0, 0, :, pcol] = (jnp.where(first_q, y_d[0], y_d[1])
                                    + y_off[:, pr * pw:(pr + 1) * pw] * jnp.where(first_q, e_ac[0], e_ac[1]))
            xw.append((xs_p * jnp.where(first_q, w_st[0], w_st[1])).astype(BF16))
            dec.append(jnp.where(first_1, e_al[0], e_al[1]))
        s_new = lax.dot_general(bg, jnp.concatenate(xw, axis=1), (((0,), (0,)), ((), ())),
                                preferred_element_type=F32)
        state_ref[g] = h_prev * jnp.concatenate(dec, axis=1) + s_new


def ssd_scan_joint(xbc, dt, dtt, a_row, a_col, n_ctx_chunks):
    bsz, ln, _ = xbc.shape
    nc = ln // SSD_CHUNK
    q = SSD_CHUNK

    def pos(d, c):
        back = jnp.where(c < n_ctx_chunks, n_ctx_chunks - 1 - c, nc + n_ctx_chunks - 1 - c)
        return jnp.where(d == 0, c, back)

    return pl.pallas_call(
        _ssd_kernel,
        grid=(2, bsz, nc),
        in_specs=[pl.BlockSpec((1, q, SSD_WIDTH), lambda d, b, c: (b, pos(d, c), 0)),
                  pl.BlockSpec((1, q, SSD_GN), lambda d, b, c: (b, pos(d, c), SSD_WIDTH // SSD_GN)),
                  pl.BlockSpec((1, q, SSD_GN), lambda d, b, c: (b, pos(d, c), SSD_WIDTH // SSD_GN + 1)),
                  pl.BlockSpec((1, 1, q, SSD_HEADS), lambda d, b, c: (d, b, pos(d, c), 0)),
                  pl.BlockSpec((1, 1, SSD_HEADS, q), lambda d, b, c: (d, b, 0, pos(d, c))),
                  pl.BlockSpec((1, 1, SSD_HEADS), lambda d, b, c: (d, 0, 0)),
                  pl.BlockSpec((1, SSD_HEADS, 1), lambda d, b, c: (d, 0, 0))],
        out_specs=pl.BlockSpec((1, 1, q, SSD_WIDTH), lambda d, b, c: (d, b, pos(d, c), 0)),
        out_shape=jax.ShapeDtypeStruct((2, bsz, ln, SSD_WIDTH), F32),
        scratch_shapes=[pltpu.VMEM((SSD_GROUPS, SSD_STATE, SSD_GW), F32)],
        compiler_params=_cparams(("parallel", "parallel", "arbitrary")),
        name="ssd_scan",
    )(xbc, xbc, xbc, dt, dtt, a_row, a_col)


EO_ROWS = 16


def _even_out_kernel(tq, yf_ref, yb_ref, xs_ref, zs_ref, cu_ref, dv_ref, ng_ref, lg_ref, lb_ref, o_ref):
    gw = SSD_WIDTH // SSD_GROUPS

    def body(r, carry):
        rows = pl.ds(pl.multiple_of(r * EO_ROWS, EO_ROWS), EO_ROWS)
        y = yf_ref[0, 0, rows, :] + yb_ref[0, 0, rows, :]
        y = y + dv_ref[...] * xs_ref[0, rows, :]
        y = y * zs_ref[rows, :]
        for g in range(SSD_GROUPS):
            sl = slice(g * gw, (g + 1) * gw)
            yg = y[:, sl]
            ms = jnp.mean(yg * yg, axis=-1, keepdims=True)
            o_ref[rows, sl] = (yg * lax.rsqrt(ms + EPS) * ng_ref[:, sl]).astype(o_ref.dtype)
        cv = cu_ref[0, rows, :]
        mu = jnp.mean(cv, axis=-1, keepdims=True)
        xc = cv - mu
        var = jnp.mean(xc * xc, axis=-1, keepdims=True)
        ln = xc * lax.rsqrt(var + EPS) * lg_ref[...] + lb_ref[...]
        o_ref[rows, SSD_WIDTH:SSD_WIDTH + CM_WIDTH] = _silu(ln).astype(o_ref.dtype)
        return carry
    lax.fori_loop(0, tq // EO_ROWS, body, 0, unroll=2)


def even_out(y, xbc, zs, cu, dvec, ng, lg, lb, *, blk0, nblk, tq):
    bsz = xbc.shape[0]
    vec = pl.BlockSpec((1, SSD_WIDTH), lambda b, i: (0, 0))
    return pl.pallas_call(
        functools.partial(_even_out_kernel, tq),
        grid=(bsz, nblk),
        in_specs=[pl.BlockSpec((1, 1, tq, SSD_WIDTH), lambda b, i: (0, b, blk0 + i, 0)),
                  pl.BlockSpec((1, 1, tq, SSD_WIDTH), lambda b, i: (1, b, blk0 + i, 0)),
                  pl.BlockSpec((1, tq, SSD_WIDTH), lambda b, i: (b, blk0 + i, 0)),
                  pl.BlockSpec((tq, SSD_WIDTH), lambda b, i: (b * nblk + i, 0)),
                  pl.BlockSpec((1, tq, CM_WIDTH), lambda b, i: (b, blk0 + i, 0)),
                  vec, vec, vec, vec],
        out_specs=pl.BlockSpec((tq, SSD_WIDTH + CM_WIDTH), lambda b, i: (b * nblk + i, 0)),
        out_shape=jax.ShapeDtypeStruct((bsz * nblk * tq, SSD_WIDTH + CM_WIDTH), BF16),
        compiler_params=_cparams(("parallel", "parallel")),
        name="even_mixer_tail",
    )(y, y, xbc, zs, cu, dvec, ng, lg, lb)


_NT = (((1,), (1,)), ((), ()))


def _na_kernel(rows_total, q_ref, k_ref, v_ref, kc_ref, vc_ref, bias_ref, o_ref):
    rb = pl.program_id(2)
    k0 = jnp.clip(rb * NA_QROWS - NA_ROWS // 2, 0, rows_total - NA_KROWS) * GRID_W
    k0 = pl.multiple_of(k0, GRID_W)
    nk = NA_KROWS * GRID_W
    for h in range(NA_HPS):
        hc = slice(h * NA_HEAD_DIM, (h + 1) * NA_HEAD_DIM)
        qv = q_ref[:, hc]
        s = lax.dot_general(qv, k_ref[pl.ds(k0, nk), hc], _NT, preferred_element_type=F32) + bias_ref[0, h]
        sc = lax.dot_general(qv, kc_ref[:, hc], _NT, preferred_element_type=F32)
        m = jnp.maximum(jnp.max(s, axis=-1, keepdims=True), jnp.max(sc, axis=-1, keepdims=True))
        p = jnp.exp(s - m)
        pc = jnp.exp(sc - m)
        den = jnp.sum(p, axis=-1, keepdims=True) + jnp.sum(pc, axis=-1, keepdims=True)
        o = jnp.dot(p.astype(BF16), v_ref[pl.ds(k0, nk), hc], preferred_element_type=F32)
        o = o + jnp.dot(pc.astype(BF16), vc_ref[:, hc], preferred_element_type=F32)
        o_ref[:, hc] = (o / den).astype(o_ref.dtype)


def neighbourhood_attention(q, k, v, kc, vc, bias, bsz, seq, n_ctx):
    rows = seq // GRID_W
    nrb = rows // NA_QROWS
    tq = NA_QROWS * GRID_W
    nk = NA_KROWS * GRID_W
    hw = NA_HPS * NA_HEAD_DIM

    def btype(rb):
        return jnp.where(rb == 0, 0, jnp.where(rb == nrb - 1, 2, 1))

    return pl.pallas_call(
        functools.partial(_na_kernel, rows),
        grid=(NA_HEADS // NA_HPS, bsz, nrb),
        in_specs=[pl.BlockSpec((tq, hw), lambda h, b, r: (b * nrb + r, h)),
                  pl.BlockSpec((seq, hw), lambda h, b, r: (b, h)),
                  pl.BlockSpec((seq, hw), lambda h, b, r: (b, h)),
                  pl.BlockSpec((n_ctx, hw), lambda h, b, r: (b, h)),
                  pl.BlockSpec((n_ctx, hw), lambda h, b, r: (b, h)),
                  pl.BlockSpec((1, NA_HPS, tq, nk), lambda h, b, r: (btype(r), h, 0, 0))],
        out_specs=pl.BlockSpec((tq, hw), lambda h, b, r: (b * nrb + r, h)),
        out_shape=jax.ShapeDtypeStruct((bsz * seq, NA_WIDTH), BF16),
        compiler_params=_cparams(("parallel", "parallel", "arbitrary")),
        name="neighbourhood_attention",
    )(q, k, v, kc, vc, bias)


def _ctx_attn_kernel(q_ref, k_ref, v_ref, o_ref):
    s = lax.dot_general(q_ref[...], k_ref[...], _NT, preferred_element_type=F32)
    m = jnp.max(s, axis=-1, keepdims=True)
    p = jnp.exp(s - m)
    den = jnp.sum(p, axis=-1, keepdims=True)
    o = jnp.dot(p.astype(BF16), v_ref[...], preferred_element_type=F32)
    o_ref[...] = (o / den).astype(o_ref.dtype)


def context_attention(q, k, v, bsz, n_ctx):
    spec = pl.BlockSpec((n_ctx, NA_HEAD_DIM), lambda b, h: (b, h))
    return pl.pallas_call(
        _ctx_attn_kernel,
        grid=(bsz, NA_HEADS),
        in_specs=[spec, spec, spec],
        out_specs=spec,
        out_shape=jax.ShapeDtypeStruct((bsz * n_ctx, NA_WIDTH), BF16),
        compiler_params=_cparams(("parallel", "parallel")),
        name="context_attention",
    )(q, k, v)


def _na_bias_tables(rpb, rows):
    nrb = rows // NA_QROWS
    col = np.arange(GRID_W)
    c0 = np.clip(col - NA_COLS // 2, 0, GRID_W - NA_COLS)
    col_ok = (col[None, :] >= c0[:, None]) & (col[None, :] < c0[:, None] + NA_COLS)
    col_idx = np.clip(col[None, :] - col[:, None] + NA_COLS - 1, 0, 2 * NA_COLS - 2)
    kr = min(NA_ROWS, rows)
    ridx = np.zeros((3, NA_QROWS, NA_KROWS), np.int32)
    rok = np.zeros((3, NA_QROWS, NA_KROWS), bool)
    for t, rb in enumerate((0, 1, nrb - 1)):
        k0 = int(np.clip(rb * NA_QROWS - NA_ROWS // 2, 0, rows - NA_KROWS))
        for rq in range(NA_QROWS):
            r = rb * NA_QROWS + rq
            r0 = int(np.clip(r - kr // 2, 0, rows - kr))
            for rk in range(NA_KROWS):
                ra = k0 + rk
                rok[t, rq, rk] = r0 <= ra < r0 + kr
                ridx[t, rq, rk] = int(np.clip(ra - r + NA_ROWS - 1, 0, 2 * NA_ROWS - 2))
    n_off = 2 * NA_ROWS - 1
    tz = rpb[:, :, col_idx].astype(F32)
    tz = jnp.where(col_ok[None, None], tz, NEG_BIG)
    tz = jnp.concatenate([tz, jnp.full((NA_HEADS, 1, GRID_W, GRID_W), NEG_BIG, F32)], axis=1)
    tile_of = np.where(rok, ridx, n_off)

    def build(tz_ref, o_ref):
        for t in range(3):
            @pl.when(pl.program_id(0) == t)
            def _(t=t):
                for rq in range(NA_QROWS):
                    for pr in range(NA_KROWS // 2):
                        a, b = int(tile_of[t, rq, 2 * pr]), int(tile_of[t, rq, 2 * pr + 1])
                        o_ref[0, 0, rq * GRID_W:(rq + 1) * GRID_W, 2 * pr * GRID_W:(2 * pr + 2) * GRID_W] = (
                            jnp.concatenate([tz_ref[0, a], tz_ref[0, b]], axis=1))

    return pl.pallas_call(
        build,
        grid=(3, NA_HEADS),
        in_specs=[pl.BlockSpec((1, n_off + 1, GRID_W, GRID_W), lambda t, h: (h, 0, 0, 0))],
        out_specs=pl.BlockSpec((1, 1, NA_QROWS * GRID_W, NA_KROWS * GRID_W), lambda t, h: (t, h, 0, 0)),
        out_shape=jax.ShapeDtypeStruct((3, NA_HEADS, NA_QROWS * GRID_W, NA_KROWS * GRID_W), F32),
        compiler_params=_cparams(("parallel", "parallel")),
        name="na_bias_tables",
    )(tz)


class _Stream:
    def __init__(self, mod, rows_per_mod, tm):
        self.mod = mod
        self.rows_per_mod = rows_per_mod
        self.tm = tm

    def m(self, idx):
        return self.mod[:, idx][:, None, :]


def _ffn_half(hs, sts, g, w_in, w_out, j):
    xns = [adaln(h, st.m(3 * j), st.m(3 * j + 1), g, rows_per_mod=st.rows_per_mod) for h, st in zip(hs, sts)]
    hids = [proj_matmul([xn], [w_in, w_in], [0, FFN_DIM], FFN_DIM, _ep_swiglu, BF16, tms=[st.tm], tn=512,
                        name="ffn_in")[0] for xn, st in zip(xns, sts)]
    return [matmul_residual([hid], [w_out + (0, FFN_DIM)], [0], h, st.m(3 * j + 2), 0.5,
                            rows_per_mod=st.rows_per_mod, tm=st.tm, tn=512, name="ffn_out")
            for hid, h, st in zip(hids, hs, sts)]


def _proj(xns, sts, w, names, n_out, ep, dtype, extras=(), name="mixer_in"):
    return [proj_matmul([xn], [w[n][0] for n in names], [w[n][1] for n in names], n_out, ep, dtype,
                        tms=[st.tm], tn=w[names[0]][2], extras=extras, w_transposed=w["transposed"], name=name)[0]
            for xn, st in zip(xns, sts)]


def _mixer_even(x, cx, sx, sc, g, p, bsz, seq, n_ctx):
    w = p["w_in"]
    sts = (sx, sc)
    xns = [adaln(h, st.m(3), st.m(4), g, rows_per_mod=st.rows_per_mod) for h, st in zip((x, cx), sts)]
    zs_x, zs_c = _proj(xns, sts, w, ("z",), SSD_WIDTH, _ep_silu, F32, name="ssd_z")
    xbc_x, xbc_c = _proj(xns, sts, w, ("xbc",), SSD_XBC, _ep_id, F32, name="ssd_xbc")
    dt_x, dt_c = _proj(xns, sts, w, ("dt",), 2 * SSD_HEADS, _ep_softplus_bias, F32, extras=(p["dt_bias"],),
                       name="ssd_dt")
    u_x, u_c = _proj(xns, sts, w, ("ga", "gg"), CM_WIDTH, _ep_glu_sigmoid, F32, name="conformer_glu")
    xbc = conv3_silu_joint(xbc_c.reshape(bsz, n_ctx, SSD_XBC), xbc_x.reshape(bsz, seq, SSD_XBC),
                           p["conv_w"], p["conv_b"])
    dt = jnp.concatenate([dt_c.reshape(bsz, n_ctx, 2, SSD_HEADS), dt_x.reshape(bsz, seq, 2, SSD_HEADS)], axis=1)
    dt = jnp.transpose(dt, (2, 0, 1, 3))
    dtt = jnp.transpose(dt, (0, 1, 3, 2))
    y = ssd_scan_joint(xbc, dt, dtt, p["a_row"], p["a_col"], n_ctx // SSD_CHUNK)
    cu = conv31_joint(u_c.reshape(bsz, n_ctx, CM_WIDTH), u_x.reshape(bsz, seq, CM_WIDTH), p["cm_w"], p["cm_b"])
    tq = n_ctx
    tail = functools.partial(even_out, y, xbc, dvec=p["dvec"], ng=p["norm_g"], lg=p["ln_g"], lb=p["ln_b"], tq=tq)
    lhs_x = tail(zs=zs_x, cu=cu, blk0=1, nblk=seq // tq)
    lhs_c = tail(zs=zs_c, cu=cu, blk0=0, nblk=1)
    return lhs_x, lhs_c


def _mixer_odd(x, cx, sx, sc, g, p, bsz, seq, n_ctx, need_ctx):
    w = p["w_in"]
    hg_q, hg_k = p["q_g"], p["k_g"]
    ep_q = functools.partial(_ep_head_rms, NA_HEAD_DIM ** -0.5)
    ep_k = functools.partial(_ep_head_rms, None)
    xn_x = adaln(x, sx.m(3), sx.m(4), g, rows_per_mod=sx.rows_per_mod)
    xn_c = adaln(cx, sc.m(3), sc.m(4), g, rows_per_mod=sc.rows_per_mod)
    both, xns = (sx, sc), (xn_x, xn_c)
    some = (both, xns) if need_ctx else ((sx,), (xn_x,))
    k, kc = _proj(xns, both, w, ("k",), NA_WIDTH, ep_k, BF16, extras=(hg_k,), name="na_k")
    v, vc = _proj(xns, both, w, ("v",), NA_WIDTH, _ep_id, BF16, name="na_v")
    q = _proj(some[1], some[0], w, ("q",), NA_WIDTH, ep_q, BF16, extras=(hg_q,), name="na_q")
    gb = _proj(some[1], some[0], w, ("gb",), SC_WIDTH, _ep_id, F32, name="sc_gate")
    pr = _proj(some[1], some[0], w, ("gc", "hs"), SC_WIDTH, _ep_mul, F32, name="sc_prod")
    ox = neighbourhood_attention(q[0], k, v, kc, vc, p["bias"], bsz, seq, n_ctx)
    yx = gated_conv3(pr[0].reshape(bsz, seq, SC_WIDTH), gb[0].reshape(bsz, seq, SC_WIDTH), p["sc_w"])
    yx = yx.reshape(bsz * seq, SC_WIDTH)
    if not need_ctx:
        return (ox, yx), None
    qc, gbc, prc = q[1], gb[1], pr[1]
    oc = context_attention(qc, kc, vc, bsz, n_ctx)
    yc = gated_conv3(prc.reshape(bsz, n_ctx, SC_WIDTH), gbc.reshape(bsz, n_ctx, SC_WIDTH), p["sc_w"])
    yc = yc.reshape(bsz * n_ctx, SC_WIDTH)
    return (ox, yx), (oc, yc)


def _segments(w, prefix, names, sizes, tiles, transposed=False):
    out, acc = {"transposed": transposed}, 0
    for n, s, tn in zip(names, sizes, tiles):
        if acc % tn == 0 and (transposed or s % LANES == 0):
            out[n] = ((w, prefix), acc, tn)
        elif transposed:
            out[n] = ((w[prefix + (slice(acc, acc + s),)], ()), 0, tn)
        else:
            out[n] = ((w[prefix][:, acc:acc + s], ()), 0, tn)
        acc += s
    return out


def kernel(x, c, ctx, c_ctx, w_mod, b_mod, norm_g, ffn_w_in, ffn_w_out, ev_w_in, ev_w_out, ssd_conv_w, ssd_conv_b,
           ssd_dt_bias, ssd_a_log, ssd_d, ssd_norm_g, cm_conv_w, cm_conv_b, cm_ln_g, cm_ln_b, od_w_in, od_w_out,
           na_q_g, na_k_g, na_rpb, sc_conv_w):
    bsz, seq, d = x.shape
    n_ctx = ctx.shape[1]
    rows = seq // GRID_W

    cvec = jnp.concatenate([c, c_ctx[None, :], jnp.zeros((8 - bsz - 1, d), F32)], axis=0)
    mod = modulation(cvec, w_mod, b_mod).reshape(DEPTH, 8, N_MOD, d)

    ffn_out_b = ffn_w_out.astype(BF16)
    ev_out_b = ev_w_out.astype(BF16)
    od_out_b = od_w_out.astype(BF16)
    ev_w_in_t = jnp.swapaxes(ev_w_in, 1, 2)

    xs = x.reshape(bsz * seq, d)
    cs = ctx.reshape(bsz * n_ctx, d)
    for i in range(DEPTH):
        last = i == DEPTH - 1
        sx = _Stream(mod[i, :bsz], seq, 1024)
        sc = _Stream(mod[i, bsz:bsz + 1], bsz * n_ctx, bsz * n_ctx)
        g = norm_g[i]
        xs, cs = _ffn_half((xs, cs), (sx, sc), g[0:1], (ffn_w_in, (i, 0)), (ffn_out_b, (i, 0)), 0)
        j = i // 2
        if i % 2 == 0:
            names = ("z", "xbc", "dt", "ga", "gg")
            sizes = (SSD_WIDTH, SSD_XBC, 2 * SSD_HEADS, CM_WIDTH, CM_WIDTH)
            a_neg = -jnp.exp(ssd_a_log[j].astype(F32))
            p = {
                "w_in": _segments(ev_w_in_t, (j,), names, sizes, (1024, 1024, 2 * SSD_HEADS, 512, 512), True),
                "dt_bias": ssd_dt_bias[j].reshape(1, 2 * SSD_HEADS).astype(F32),
                "conv_w": ssd_conv_w[j], "conv_b": ssd_conv_b[j][None, :],
                "a_row": a_neg[:, None, :], "a_col": a_neg[:, :, None],
                "cm_w": cm_conv_w[j], "cm_b": cm_conv_b[j][None, :],
                "dvec": jnp.repeat(ssd_d[j], SSD_HEAD_DIM)[None, :],
                "norm_g": ssd_norm_g[j][None, :], "ln_g": cm_ln_g[j][None, :], "ln_b": cm_ln_b[j][None, :],
            }
            lhs_x, lhs_c = _mixer_even(xs, cs, sx, sc, g[1:2], p, bsz, seq, n_ctx)
            w_o = [(ev_out_b, (j,), 0, SSD_WIDTH + CM_WIDTH)]
            xs = matmul_residual([lhs_x], w_o, [0], xs, sx.m(5), 1.0,
                                 rows_per_mod=sx.rows_per_mod, tm=sx.tm, tn=512, name="mixer_out")
            cs_mix = [lhs_c], w_o, [0]
        else:
            names = ("q", "k", "v", "gb", "gc", "hs")
            p = {
                "w_in": _segments(od_w_in, (j,), names, (NA_WIDTH,) * 3 + (SC_WIDTH,) * 3, (1024,) * 4 + (512,) * 2),
                "q_g": na_q_g[j][None, :], "k_g": na_k_g[j][None, :],
                "bias": _na_bias_tables(na_rpb[j], rows), "sc_w": sc_conv_w[j],
            }
            lx, lc = _mixer_odd(xs, cs, sx, sc, g[1:2], p, bsz, seq, n_ctx, not last)
            w_o = [(od_out_b, (j,), 0, NA_WIDTH), (od_out_b, (j,), 1, SC_WIDTH)]
            xs = matmul_residual(list(lx), w_o, [0, 0], xs, sx.m(5), 1.0,
                                 rows_per_mod=sx.rows_per_mod, tm=sx.tm, tn=512, name="mixer_out")
            cs_mix = (list(lc), w_o, [0, 0]) if lc is not None else None
        if last:
            (xs,) = _ffn_half((xs,), (sx,), g[2:3], (ffn_w_in, (i, 1)), (ffn_out_b, (i, 1)), 2)
        else:
            cs = matmul_residual(cs_mix[0], cs_mix[1], cs_mix[2], cs, sc.m(5), 1.0,
                                 rows_per_mod=sc.rows_per_mod, tm=sc.tm, tn=512, name="mixer_out")
            xs, cs = _ffn_half((xs, cs), (sx, sc), g[2:3], (ffn_w_in, (i, 1)), (ffn_out_b, (i, 1)), 2)
    return xs.reshape(bsz, seq, d)
```
